```python
import math
import jax, jax.numpy as jnp
from jax import lax
import numpy as np

D_MODEL = 2048
BATCH = 2
SEQ = 16384
DEPTH = 1
DEC_BATCH = 2
DEC_SEQ = 4096
PAST_LEN = 128

PLE_DIM = 256
ROPE_THETA = 10000.0
RMS_EPS = 1e-6
NEG = -1e30
DIFF_HEADS = 4
DIFF_QK_DIM = 64
DIFF_V_DIM = 128
DIFF_WIDTH = DIFF_HEADS * DIFF_V_DIM
DIFF_QK_WIDTH = DIFF_HEADS * 2 * DIFF_QK_DIM
DIFF_QBLOCK = 128
DIL_PATTERNS = ((128, 1), (512, 4), (2048, 16))
DIL_GROUPS = len(DIL_PATTERNS)
DIL_HEADS = 4
DIL_HEAD_DIM = 128
DIL_QKV_WIDTH = DIL_GROUPS * DIL_HEADS * DIL_HEAD_DIM
DIL_OUT_WIDTH = DIL_HEADS * DIL_HEAD_DIM
IN_SPLITS = (DIFF_QK_WIDTH, DIFF_QK_WIDTH, DIFF_WIDTH, DIFF_WIDTH,
             DIL_QKV_WIDTH, DIL_QKV_WIDTH, DIL_QKV_WIDTH, DIL_OUT_WIDTH)
IN_WIDTH = sum(IN_SPLITS)
MIX_OUT_WIDTH = DIFF_WIDTH + DIL_OUT_WIDTH

kernel_name = "hymba_diff_dilated_encoder"


def rmsnorm(x, g):
    xf = x.astype(jnp.float32)
    y = xf * lax.rsqrt(jnp.mean(xf * xf, axis=-1, keepdims=True) + RMS_EPS)
    return (y * g.astype(jnp.float32)).astype(x.dtype)


def rope(x):
    S, D = x.shape[1], x.shape[-1]
    inv = ROPE_THETA ** (-jnp.arange(0, D, 2, dtype=jnp.float32) / D)
    ang = jnp.arange(S, dtype=jnp.float32)[:, None] * inv[None, :]
    ang = jnp.concatenate([ang, ang], axis=-1).reshape((1, S) + (1,) * (x.ndim - 3) + (D,))
    xf = x.astype(jnp.float32)
    x1, x2 = jnp.split(xf, 2, axis=-1)
    rot = jnp.concatenate([-x2, x1], axis=-1)
    return (xf * jnp.cos(ang) + rot * jnp.sin(ang)).astype(x.dtype)


def diff_attention(q, k, v, lam):
    B, S, H = q.shape[0], q.shape[1], q.shape[2]
    nqb = S // DIFF_QBLOCK
    qb = q.reshape(B, nqb, DIFF_QBLOCK, H, 2, DIFF_QK_DIM).transpose(1, 0, 2, 3, 4, 5)
    scale = DIFF_QK_DIM ** -0.5

    def one_block(qblk):
        s = jnp.einsum('bqhcd,bkhcd->bhcqk', qblk, k).astype(jnp.float32) * scale
        pr = jax.nn.softmax(s, axis=-1)
        a = pr[:, :, 0] - lam * pr[:, :, 1]
        return jnp.einsum('bhqk,bkhd->bqhd', a.astype(v.dtype), v)

    o = lax.map(one_block, qb)
    return o.transpose(1, 0, 2, 3, 4).reshape(B, S, H, DIFF_V_DIM)


def dilated_window_attention(q, k, v, window, dilation):
    B, S, H, D = q.shape
    half = window // (2 * dilation)
    blk = half
    L = S // dilation
    nb = -(-L // blk)
    Lp = nb * blk

    def to_sub(t):
        return t.reshape(B, L, dilation, H, D).transpose(0, 2, 1, 3, 4)

    qs = jnp.pad(to_sub(q), ((0, 0), (0, 0), (0, Lp - L), (0, 0), (0, 0))).reshape(B, dilation, nb, blk, H, D)

    def windows(t):
        tp = jnp.pad(to_sub(t), ((0, 0), (0, 0), (half, Lp - L + half), (0, 0), (0, 0)))
        tp = tp.reshape(B, dilation, nb + 2, blk, H, D)
        return jnp.concatenate([tp[:, :, :-2], tp[:, :, 1:-1], tp[:, :, 2:]], axis=3)

    kw, vw = windows(k), windows(v)
    s = jnp.einsum('brnqhd,brnkhd->brnhqk', qs, kw).astype(jnp.float32) * (D ** -0.5)
    t = jnp.arange(blk)[:, None]
    kk = jnp.arange(3 * blk)[None, :]
    band = (kk >= t) & (kk <= t + 2 * half)
    idx = jnp.arange(nb)[:, None] * blk + jnp.arange(3 * blk)[None, :] - half
    valid = (idx >= 0) & (idx < L)
    mask = band[None] & valid[:, None, :]
    s = jnp.where(mask[None, None, :, None], s, NEG)
    m = jnp.max(s, axis=-1, keepdims=True)
    e = jnp.exp(s - m)
    den = jnp.sum(e, axis=-1, keepdims=True)
    o = jnp.einsum('brnhqk,brnkhd->brnqhd', (e / den).astype(v.dtype), vw)
    lse = (m + jnp.log(den))[..., 0]
    o = o.reshape(B, dilation, Lp, H, D)[:, :, :L].transpose(0, 2, 1, 3, 4).reshape(B, S, H, D)
    lse = lse.transpose(0, 1, 2, 4, 3).reshape(B, dilation, Lp, H)[:, :, :L]
    lse = lse.transpose(0, 2, 1, 3).reshape(B, S, H)
    return o, lse


def hybrid_layer(h, p_i, layer_idx, norm_mix, w_in, lam_q1, lam_k1, lam_q2, lam_k2,
                 subln, w_out, ple_norm, w_ple_gate, w_ple_proj):
    B, S, _ = h.shape
    u = rmsnorm(h, norm_mix)
    z = u @ w_in
    split_idx = np.cumsum(IN_SPLITS)[:-1].tolist()
    aq, ak, av, ag, bq, bk, bv, bg = jnp.split(z, split_idx, axis=-1)

    aq = rope(aq.reshape(B, S, DIFF_HEADS, 2, DIFF_QK_DIM))
    ak = rope(ak.reshape(B, S, DIFF_HEADS, 2, DIFF_QK_DIM))
    av = av.reshape(B, S, DIFF_HEADS, DIFF_V_DIM)
    lam_init = 0.8 - 0.6 * math.exp(-0.3 * layer_idx)
    lam = (jnp.exp(jnp.sum(lam_q1.astype(jnp.float32) * lam_k1.astype(jnp.float32)))
           - jnp.exp(jnp.sum(lam_q2.astype(jnp.float32) * lam_k2.astype(jnp.float32))) + lam_init)
    oa = diff_attention(aq, ak, av, lam)
    oa = rmsnorm(oa, subln) * (1.0 - lam_init)
    ya = oa.reshape(B, S, DIFF_WIDTH) * jax.nn.silu(ag)

    bq = rope(bq.reshape(B, S, DIL_GROUPS, DIL_HEADS, DIL_HEAD_DIM))
    bk = rope(bk.reshape(B, S, DIL_GROUPS, DIL_HEADS, DIL_HEAD_DIM))
    bv = bv.reshape(B, S, DIL_GROUPS, DIL_HEADS, DIL_HEAD_DIM)
    outs, lses = [], []
    for g, (window, dilation) in enumerate(DIL_PATTERNS):
        o_g, lse_g = dilated_window_attention(bq[:, :, g], bk[:, :, g], bv[:, :, g], window, dilation)
        outs.append(o_g)
        lses.append(lse_g)
    wts = jax.nn.softmax(jnp.stack(lses, axis=0), axis=0)
    ob = jnp.sum(wts[..., None] * jnp.stack(outs, axis=0).astype(jnp.float32), axis=0).astype(h.dtype)
    yb = ob.reshape(B, S, DIL_OUT_WIDTH) * jax.nn.silu(bg)

    h = h + jnp.concatenate([ya, yb], axis=-1) @ w_out

    gate = jax.nn.sigmoid(rmsnorm(h, ple_norm) @ w_ple_gate)
    h = h + gate * (p_i @ w_ple_proj)
    return h


def setup_inputs(seed: int = 0) -> dict:
    key = jax.random.key(seed)
    ks = jax.random.split(key, 16)
    f32 = jnp.float32
    nrm = lambda k, shape, s: jax.random.normal(k, shape, f32) * s
    return {
        "x_prompt": nrm(ks[0], (BATCH, SEQ, D_MODEL), 1.0),
        "x_sample": nrm(ks[1], (DEC_BATCH, DEC_SEQ, D_MODEL), 1.0),
        "p_prompt": nrm(ks[2], (DEPTH, BATCH, SEQ, PLE_DIM), 1.0),
        "p_sample": nrm(ks[3], (DEPTH, DEC_BATCH, DEC_SEQ, PLE_DIM), 1.0),
        "norm_mix": 1.0 + nrm(ks[4], (DEPTH, D_MODEL), 0.01),
        "w_in": nrm(ks[5], (DEPTH, D_MODEL, IN_WIDTH), D_MODEL ** -0.5),
        "lam_q1": nrm(ks[6], (DEPTH, DIFF_QK_DIM), 0.1),
        "lam_k1": nrm(ks[7], (DEPTH, DIFF_QK_DIM), 0.1),
        "lam_q2": nrm(ks[8], (DEPTH, DIFF_QK_DIM), 0.1),
        "lam_k2": nrm(ks[9], (DEPTH, DIFF_QK_DIM), 0.1),
        "subln": 1.0 + nrm(ks[10], (DEPTH, DIFF_V_DIM), 0.01),
        "w_out": nrm(ks[11], (DEPTH, MIX_OUT_WIDTH, D_MODEL), MIX_OUT_WIDTH ** -0.5),
        "ple_norm": 1.0 + nrm(ks[12], (DEPTH, D_MODEL), 0.01),
        "w_ple_gate": nrm(ks[13], (DEPTH, D_MODEL, D_MODEL), D_MODEL ** -0.5),
        "w_ple_proj": nrm(ks[14], (DEPTH, PLE_DIM, D_MODEL), PLE_DIM ** -0.5),
        "final_norm": 1.0 + nrm(ks[15], (D_MODEL,), 0.01),
    }


def reference(x_prompt, x_sample, p_prompt, p_sample, norm_mix, w_in, lam_q1, lam_k1,
              lam_q2, lam_k2, subln, w_out, ple_norm, w_ple_gate, w_ple_proj, final_norm):
    def trunk(x, p):
        h = x
        for i in range(DEPTH):
            h = hybrid_layer(h, p[i], i, norm_mix[i], w_in[i], lam_q1[i], lam_k1[i],
                             lam_q2[i], lam_k2[i], subln[i], w_out[i], ple_norm[i],
                             w_ple_gate[i], w_ple_proj[i])
        return rmsnorm(h, final_norm)

    y_prompt = trunk(x_prompt, p_prompt)
    y_sample = trunk(x_sample, p_sample)
    return (y_prompt, y_sample)
```

```python
import functools
import math

import jax
import jax.numpy as jnp
from jax import lax
from jax.experimental import pallas as pl
from jax.experimental.pallas import tpu as pltpu

F32 = jnp.float32
BF16 = jnp.bfloat16

D_MODEL = 2048
PLE_DIM = 256
ROPE_THETA = 10000.0
RMS_EPS = 1e-6
NEG = -1e30
LOG2E = math.log2(math.e)
LN2 = math.log(2.0)

DIFF_HEADS = 4
DIFF_QK_DIM = 64
DIFF_V_DIM = 128
DIFF_WIDTH = DIFF_HEADS * DIFF_V_DIM
DIL_PATTERNS = ((128, 1), (512, 4), (2048, 16))
DIL_HEADS = 4
DIL_HEAD_DIM = 128
DIL_WIDTH = DIL_HEADS * DIL_HEAD_DIM
DIL_RADIUS = 64
assert all(w // (2 * d) == DIL_RADIUS for w, d in DIL_PATTERNS)

LANES = 128
VMEM_LIMIT_BYTES = 56 * 1024 * 1024


def _cparams(semantics):
    return pltpu.CompilerParams(dimension_semantics=semantics, vmem_limit_bytes=VMEM_LIMIT_BYTES)


def _rope_tables(seq):
    pos = jnp.arange(seq, dtype=F32)[:, None]
    inv64 = ROPE_THETA ** (-jnp.arange(0, DIFF_QK_DIM, 2, dtype=F32) / DIFF_QK_DIM)
    ang = pos * inv64[None, :]
    c, s, z = jnp.cos(ang), jnp.sin(ang), jnp.zeros_like(ang)
    cos_a = jnp.concatenate([c, c, c, c], axis=-1)
    sin_a_up = jnp.concatenate([-s, z, -s, z], axis=-1)
    sin_a_dn = jnp.concatenate([z, s, z, s], axis=-1)
    inv128 = ROPE_THETA ** (-jnp.arange(0, DIL_HEAD_DIM, 2, dtype=F32) / DIL_HEAD_DIM)
    ang = pos * inv128[None, :]
    c, s = jnp.cos(ang), jnp.sin(ang)
    cos_b = jnp.concatenate([c, c], axis=-1)
    sin_b = jnp.concatenate([-s, s], axis=-1)
    return (cos_a, sin_a_up, sin_a_dn), (cos_b, sin_b)


def _rope64(x, cos, sin_up, sin_dn):
    return x * cos + pltpu.roll(x, 96, 1) * sin_up + pltpu.roll(x, 32, 1) * sin_dn


def _rope128(x, cos, sin):
    return x * cos + pltpu.roll(x, 64, 1) * sin


def _norm_matmul(x_ref, g_ref, w_ref):
    x = x_ref[...]
    ms = jnp.mean(x * x, axis=-1, keepdims=True)
    u = (x * lax.rsqrt(ms + RMS_EPS) * g_ref[...]).astype(BF16)
    return jnp.dot(u, w_ref[...], preferred_element_type=F32)


def _proj_diff_kernel(x_ref, g_ref, w_ref, cos_ref, sup_ref, sdn_ref, qt_ref, k_ref, vt_ref, gate_ref):
    z = _norm_matmul(x_ref, g_ref, w_ref)
    cos, sup, sdn = cos_ref[...], sup_ref[...], sdn_ref[...]
    qscale = DIFF_QK_DIM**-0.5 * LOG2E
    for h in range(DIFF_HEADS):
        lo = h * LANES
        q = _rope64(z[:, lo:lo + LANES], cos, sup, sdn) * qscale
        qt_ref[lo:lo + LANES, :] = q.T.astype(BF16)
        k = _rope64(z[:, DIFF_WIDTH + lo:DIFF_WIDTH + lo + LANES], cos, sup, sdn)
        k_ref[:, lo:lo + LANES] = k.astype(BF16)
        v = z[:, 2 * DIFF_WIDTH + lo:2 * DIFF_WIDTH + lo + LANES]
        vt_ref[lo:lo + LANES, :] = v.T.astype(BF16)
    gate_ref[...] = z[:, 3 * DIFF_WIDTH:]


def _proj_dil_kernel(x_ref, g_ref, w_ref, cos_ref, sin_ref, *refs, dilation, with_gate):
    if with_gate:
        q_ref, k_ref, v_ref, gate_ref = refs[:4]
        refs = refs[4:]
    else:
        q_ref, k_ref, v_ref = refs[:3]
        refs = refs[3:]
    z = _norm_matmul(x_ref, g_ref, w_ref)
    cos, sin = cos_ref[...], sin_ref[...]
    qscale = DIL_HEAD_DIM**-0.5 * LOG2E
    tm = z.shape[0]
    rows = tm // dilation
    if dilation == 1:
        for h in range(DIL_HEADS):
            lo = h * LANES
            q_ref[0, :, lo:lo + LANES] = (_rope128(z[:, lo:lo + LANES], cos, sin) * qscale).astype(BF16)
            k_ref[0, :, lo:lo + LANES] = _rope128(
                z[:, DIL_WIDTH + lo:DIL_WIDTH + lo + LANES], cos, sin).astype(BF16)
        v_ref[0] = z[:, 2 * DIL_WIDTH:3 * DIL_WIDTH].astype(BF16)
    else:
        (stage_ref,) = refs
        for h in range(DIL_HEADS):
            lo = h * LANES
            stage_ref[h] = _rope128(z[:, lo:lo + LANES], cos, sin) * qscale
            stage_ref[DIL_HEADS + h] = _rope128(z[:, DIL_WIDTH + lo:DIL_WIDTH + lo + LANES], cos, sin)
            stage_ref[2 * DIL_HEADS + h] = z[:, 2 * DIL_WIDTH + lo:2 * DIL_WIDTH + lo + LANES]
        for r in range(dilation):
            for j, out_ref in enumerate((q_ref, k_ref, v_ref)):
                for h in range(DIL_HEADS):
                    blk = stage_ref[j * DIL_HEADS + h, pl.ds(r, rows, stride=dilation), :]
                    out_ref[r, :, h * LANES:(h + 1) * LANES] = blk.astype(BF16)
    if with_gate:
        gate_ref[...] = z[:, 3 * DIL_WIDTH:]


def _proj_tile(seq):
    return min(512, seq)


def _proj_diff(x, g, w, tabs):
    batch, seq, _ = x.shape
    tm = _proj_tile(seq)
    row = lambda b, i: (b, i, 0)
    col = lambda b, i: (b, 0, i)
    tab_spec = pl.BlockSpec((tm, LANES), lambda b, i: (i, 0))
    return pl.pallas_call(
        _proj_diff_kernel,
        grid=(batch, seq // tm),
        in_specs=[
            pl.BlockSpec((None, tm, D_MODEL), row),
            pl.BlockSpec((1, D_MODEL), lambda b, i: (0, 0)),
            pl.BlockSpec(w.shape, lambda b, i: (0, 0)),
            tab_spec, tab_spec, tab_spec,
        ],
        out_specs=[
            pl.BlockSpec((None, DIFF_WIDTH, tm), col),
            pl.BlockSpec((None, tm, DIFF_WIDTH), row),
            pl.BlockSpec((None, DIFF_WIDTH, tm), col),
            pl.BlockSpec((None, tm, DIFF_WIDTH), row),
        ],
        out_shape=[
            jax.ShapeDtypeStruct((batch, DIFF_WIDTH, seq), BF16),
            jax.ShapeDtypeStruct((batch, seq, DIFF_WIDTH), BF16),
            jax.ShapeDtypeStruct((batch, DIFF_WIDTH, seq), BF16),
            jax.ShapeDtypeStruct((batch, seq, DIFF_WIDTH), F32),
        ],
        compiler_params=_cparams(("parallel", "parallel")),
        name="proj_diff",
    )(x, g, w, *tabs)


def _proj_dil(x, g, w, tabs, dilation, with_gate):
    batch, seq, _ = x.shape
    tm = _proj_tile(seq)
    rows = tm // dilation
    sub = seq // dilation
    row = lambda b, i: (b, i, 0)
    res = lambda b, i: (b, 0, i, 0)
    tab_spec = pl.BlockSpec((tm, LANES), lambda b, i: (i, 0))
    res_spec = pl.BlockSpec((None, dilation, rows, DIL_WIDTH), res)
    res_shape = jax.ShapeDtypeStruct((batch, dilation, sub, DIL_WIDTH), BF16)
    out_specs = [res_spec, res_spec, res_spec]
    out_shape = [res_shape, res_shape, res_shape]
    if with_gate:
        out_specs.append(pl.BlockSpec((None, tm, DIL_WIDTH), row))
        out_shape.append(jax.ShapeDtypeStruct((batch, seq, DIL_WIDTH), F32))
    scratch = [] if dilation == 1 else [pltpu.VMEM((3 * DIL_HEADS, tm, LANES), F32)]
    return pl.pallas_call(
        functools.partial(_proj_dil_kernel, dilation=dilation, with_gate=with_gate),
        grid=(batch, seq // tm),
        in_specs=[
            pl.BlockSpec((None, tm, D_MODEL), row),
            pl.BlockSpec((1, D_MODEL), lambda b, i: (0, 0)),
            pl.BlockSpec(w.shape, lambda b, i: (0, 0)),
            tab_spec, tab_spec,
        ],
        out_specs=out_specs,
        out_shape=out_shape,
        scratch_shapes=scratch,
        compiler_params=_cparams(("parallel", "parallel")),
        name=f"proj_dil{dilation}",
    )(x, g, w, *tabs)


def _diff_attn_kernel(lam_ref, subln_ref, qt_ref, k_ref, vt_ref, gate_ref, o_ref,
                      w_scr, m_scr, l_scr, acc_scr, *, tk, lam_init):
    seq = k_ref.shape[0]
    qt = qt_ref[...]
    row = lax.broadcasted_iota(jnp.int32, qt.shape, 0)
    zero = jnp.zeros_like(qt)
    w_scr[0] = jnp.where(row < DIFF_QK_DIM, qt, zero)
    w_scr[1] = jnp.where(row >= DIFF_QK_DIM, qt, zero)
    m_scr[...] = jnp.full(m_scr.shape, NEG, F32)
    l_scr[...] = jnp.zeros(l_scr.shape, F32)
    acc_scr[...] = jnp.zeros(acc_scr.shape, F32)

    def body(ki, carry):
        start = pl.multiple_of(ki * tk, tk)
        kb = k_ref[pl.ds(start, tk), :]
        vb = vt_ref[:, pl.ds(start, tk)]
        for c in range(2):
            s = jnp.dot(kb, w_scr[c], preferred_element_type=F32)
            m_old = m_scr[c]
            m_new = jnp.maximum(m_old, jnp.max(s, axis=0, keepdims=True))
            alpha = jnp.exp2(m_old - m_new)
            p = jnp.exp2(s - m_new)
            l_scr[c] = alpha * l_scr[c] + jnp.sum(p, axis=0, keepdims=True)
            acc_scr[c] = alpha * acc_scr[c] + jnp.dot(vb, p.astype(BF16), preferred_element_type=F32)
            m_scr[c] = m_new
        return carry

    lax.fori_loop(0, seq // tk, body, 0)

    lam_p = lam_ref[...]
    lam = (jnp.exp(jnp.sum(lam_p[0:1] * lam_p[1:2], axis=-1, keepdims=True))
           - jnp.exp(jnp.sum(lam_p[2:3] * lam_p[3:4], axis=-1, keepdims=True)) + lam_init)
    o_t = acc_scr[0] * (1.0 / l_scr[0]) - lam * (acc_scr[1] * (1.0 / l_scr[1]))
    ms = jnp.mean(o_t * o_t, axis=0, keepdims=True)
    o = (o_t * lax.rsqrt(ms + RMS_EPS)).T
    g = gate_ref[...]
    silu = g / (1.0 + jnp.exp(-g))
    o_ref[...] = (o * (subln_ref[...] * (1.0 - lam_init)) * silu).astype(o_ref.dtype)


def _diff_attention(lam_params, subln, qt, k, vt, gate, lam_init):
    batch, seq, _ = k.shape
    tq = min(512, seq)
    tk = min(512, seq)
    return pl.pallas_call(
        functools.partial(_diff_attn_kernel, tk=tk, lam_init=lam_init),
        grid=(batch, DIFF_HEADS, seq // tq),
        in_specs=[
            pl.BlockSpec(lam_params.shape, lambda b, h, i: (0, 0)),
            pl.BlockSpec(subln.shape, lambda b, h, i: (0, 0)),
            pl.BlockSpec((None, LANES, tq), lambda b, h, i: (b, h, i)),
            pl.BlockSpec((None, seq, LANES), lambda b, h, i: (b, 0, h)),
            pl.BlockSpec((None, LANES, seq), lambda b, h, i: (b, h, 0)),
            pl.BlockSpec((None, tq, LANES), lambda b, h, i: (b, i, h)),
        ],
        out_specs=pl.BlockSpec((None, tq, LANES), lambda b, h, i: (b, i, h)),
        out_shape=jax.ShapeDtypeStruct((batch, seq, DIFF_WIDTH), BF16),
        scratch_shapes=[
            pltpu.VMEM((2, LANES, tq), BF16),
            pltpu.VMEM((2, 1, tq), F32),
            pltpu.VMEM((2, 1, tq), F32),
            pltpu.VMEM((2, LANES, tq), F32),
        ],
        compiler_params=_cparams(("parallel", "parallel", "parallel")),
        name="diff_attn",
    )(lam_params, subln, qt, k, vt, gate)


def _band_attn_kernel(q_ref, kp_ref, kc_ref, kn_ref, vp_ref, vc_ref, vn_ref, o_ref, lse_ref, *, sub_len):
    tq = q_ref.shape[0]
    tw = tq + 2 * DIL_RADIUS
    i = pl.program_id(1)
    kwin = jnp.concatenate([kp_ref[...], kc_ref[...], kn_ref[...]], axis=0)
    vwin = jnp.concatenate([vp_ref[...], vc_ref[...], vn_ref[...]], axis=0)
    row = lax.broadcasted_iota(jnp.int32, (tq, tw), 0)
    col = lax.broadcasted_iota(jnp.int32, (tq, tw), 1)
    rel = col - row
    kpos = i * tq - DIL_RADIUS + col
    bias = jnp.where(rel >= 0, 0.0, NEG)
    bias = jnp.where(rel <= 2 * DIL_RADIUS, bias, NEG)
    bias = jnp.where(kpos >= 0, bias, NEG)
    bias = jnp.where(kpos < sub_len, bias, NEG).astype(F32)
    for h in range(DIL_HEADS):
        lo = h * LANES
        s = lax.dot_general(q_ref[:, lo:lo + LANES], kwin[:, lo:lo + LANES],
                            (((1,), (1,)), ((), ())), preferred_element_type=F32)
        s = s + bias
        m = jnp.max(s, axis=-1, keepdims=True)
        p = jnp.exp2(s - m)
        den = jnp.sum(p, axis=-1, keepdims=True)
        o = jnp.dot(p.astype(BF16), vwin[:, lo:lo + LANES], preferred_element_type=F32)
        o_ref[:, lo:lo + LANES] = o * (1.0 / den)
        lse = (m + jnp.log2(den)) * LN2
        lse_ref[:, lo:lo + LANES] = jnp.broadcast_to(lse, (tq, LANES))


def _band_attention(q, k, v):
    nseq, sub, _ = q.shape
    tq = min(512, sub)
    per = tq // DIL_RADIUS
    nhalo = sub // DIL_RADIUS
    cur = pl.BlockSpec((None, tq, DIL_WIDTH), lambda n, i: (n, i, 0))
    prev = pl.BlockSpec((None, DIL_RADIUS, DIL_WIDTH), lambda n, i: (n, jnp.maximum(i * per - 1, 0), 0))
    nxt = pl.BlockSpec((None, DIL_RADIUS, DIL_WIDTH),
                       lambda n, i: (n, jnp.minimum((i + 1) * per, nhalo - 1), 0))
    shape = jax.ShapeDtypeStruct((nseq, sub, DIL_WIDTH), F32)
    return pl.pallas_call(
        functools.partial(_band_attn_kernel, sub_len=sub),
        grid=(nseq, sub // tq),
        in_specs=[cur, prev, cur, nxt, prev, cur, nxt],
        out_specs=[cur, cur],
        out_shape=[shape, shape],
        compiler_params=_cparams(("parallel", "parallel")),
        name=f"band_attn{sub}",
    )(q, k, k, k, v, v, v)


def _rms(x, g):
    ms = jnp.mean(x * x, axis=-1, keepdims=True)
    return x * lax.rsqrt(ms + RMS_EPS) * g


def _tail_kernel(x_ref, ya_ref, o0_ref, l0_ref, o1_ref, l1_ref, o2_ref, l2_ref, bg_ref, p_ref,
                 wout_ref, wgate_ref, wproj_ref, plen_ref, fnorm_ref, y_ref,
                 o1_scr, l1_scr, o2_scr, l2_scr, yb_scr):
    tm = x_ref.shape[0]
    for src, dst in ((o1_ref, o1_scr), (l1_ref, l1_scr), (o2_ref, o2_scr), (l2_ref, l2_scr)):
        d = src.shape[0]
        for r in range(d):
            for h in range(DIL_HEADS):
                dst[h, pl.ds(r, tm // d, stride=d), :] = src[r, :, h * LANES:(h + 1) * LANES]
    for h in range(DIL_HEADS):
        slab = slice(h * LANES, (h + 1) * LANES)
        l0, l1, l2 = l0_ref[:, slab], l1_scr[h], l2_scr[h]
        mx = jnp.maximum(jnp.maximum(l0, l1), l2)
        e0, e1, e2 = jnp.exp(l0 - mx), jnp.exp(l1 - mx), jnp.exp(l2 - mx)
        ob = (e0 * o0_ref[:, slab] + e1 * o1_scr[h] + e2 * o2_scr[h]) * (1.0 / (e0 + e1 + e2))
        bg = bg_ref[:, slab]
        yb_scr[:, slab] = (ob * (bg / (1.0 + jnp.exp(-bg)))).astype(BF16)
    h = (x_ref[...]
         + jnp.dot(ya_ref[...], wout_ref[:DIFF_WIDTH, :], preferred_element_type=F32)
         + jnp.dot(yb_scr[...], wout_ref[DIFF_WIDTH:, :], preferred_element_type=F32))
    n = _rms(h, plen_ref[...]).astype(BF16)
    gate = 1.0 / (1.0 + jnp.exp(-jnp.dot(n, wgate_ref[...], preferred_element_type=F32)))
    h = h + gate * jnp.dot(p_ref[...].astype(BF16), wproj_ref[...], preferred_element_type=F32)
    y_ref[...] = _rms(h, fnorm_ref[...])


def _tail(x, ya, o0, l0, o1, l1, o2, l2, bg, p, wout, wgate, wproj, plen, fnorm):
    batch, seq, _ = x.shape
    tm = min(256, seq)
    row = lambda b, i: (b, i, 0)
    full = lambda a: pl.BlockSpec(a.shape, lambda b, i: (0,) * a.ndim)
    wide = pl.BlockSpec((None, tm, D_MODEL), row)
    half = pl.BlockSpec((None, tm, DIL_WIDTH), row)

    def res(a):
        d = a.shape[1]
        return pl.BlockSpec((None, d, tm // d, DIL_WIDTH), lambda b, i: (b, 0, i, 0))

    return pl.pallas_call(
        _tail_kernel,
        grid=(batch, seq // tm),
        in_specs=[wide, half, half, half, res(o1), res(l1), res(o2), res(l2), half,
                  pl.BlockSpec((None, tm, PLE_DIM), row),
                  full(wout), full(wgate), full(wproj), full(plen), full(fnorm)],
        out_specs=wide,
        out_shape=jax.ShapeDtypeStruct((batch, seq, D_MODEL), F32),
        scratch_shapes=[pltpu.VMEM((DIL_HEADS, tm, LANES), F32) for _ in range(4)]
        + [pltpu.VMEM((tm, DIL_WIDTH), BF16)],
        compiler_params=_cparams(("parallel", "parallel")),
        name="tail",
    )(x, ya, o0, l0, o1, l1, o2, l2, bg, p, wout, wgate, wproj, plen, fnorm)


def _split_w_in(w_in):
    w = w_in.astype(BF16)
    a_end = 4 * DIFF_WIDTH
    groups = len(DIL_PATTERNS)
    bq, bk, bv = (a_end + j * groups * DIL_WIDTH for j in range(3))
    bg = a_end + 3 * groups * DIL_WIDTH
    w_diff = w[:, :a_end]
    w_dil = []
    for g in range(groups):
        cols = [w[:, base + g * DIL_WIDTH: base + (g + 1) * DIL_WIDTH] for base in (bq, bk, bv)]
        if g == 0:
            cols.append(w[:, bg:bg + DIL_WIDTH])
        w_dil.append(jnp.concatenate(cols, axis=1))
    return w_diff, w_dil


def _layer(x, p, layer_idx, norm_mix, w_diff, w_dil, lam_params, subln, wout, wgate, wproj, plen, fnorm):
    batch, seq, _ = x.shape
    tabs_a, tabs_b = _rope_tables(seq)
    lam_init = 0.8 - 0.6 * math.exp(-0.3 * layer_idx)

    qt, k, vt, ag = _proj_diff(x, norm_mix, w_diff, tabs_a)
    ya = _diff_attention(lam_params, subln, qt, k, vt, ag, lam_init)

    outs, lses, bg = [], [], None
    for g, (_, dilation) in enumerate(DIL_PATTERNS):
        res = _proj_dil(x, norm_mix, w_dil[g], tabs_b, dilation, with_gate=(g == 0))
        if g == 0:
            bg = res[3]
        sub = seq // dilation
        q, kk, v = (a.reshape(batch * dilation, sub, DIL_WIDTH) for a in res[:3])
        o, lse = _band_attention(q, kk, v)
        outs.append(o.reshape(batch, dilation, sub, DIL_WIDTH))
        lses.append(lse.reshape(batch, dilation, sub, DIL_WIDTH))

    return _tail(x, ya, outs[0].reshape(batch, seq, DIL_WIDTH), lses[0].reshape(batch, seq, DIL_WIDTH),
                 outs[1], lses[1], outs[2], lses[2], bg, p, wout, wgate, wproj, plen, fnorm)


def kernel(x_prompt, x_sample, p_prompt, p_sample, norm_mix, w_in, lam_q1, lam_k1, lam_q2, lam_k2,
           subln, w_out, ple_norm, w_ple_gate, w_ple_proj, final_norm):
    depth = w_in.shape[0]
    assert depth == 1, "the fused tail applies the final norm, so it handles a single layer"
    i = 0
    w_diff, w_dil = _split_w_in(w_in[i])
    lam_params = jnp.stack([lam_q1[i], lam_k1[i], lam_q2[i], lam_k2[i]], axis=0)
    args = (norm_mix[i][None, :], w_diff, w_dil, lam_params, subln[i][None, :],
            w_out[i].astype(BF16), w_ple_gate[i].astype(BF16), w_ple_proj[i].astype(BF16),
            ple_norm[i][None, :], final_norm[None, :])
    y_prompt = _layer(x_prompt, p_prompt[i], i, *args)
    y_sample = _layer(x_sample, p_sample[i], i, *args)
    return (y_prompt, y_sample)
```

```python
import functools
import math

import jax
import jax.numpy as jnp
from jax import lax
from jax.experimental import pallas as pl
from jax.experimental.pallas import tpu as pltpu

F32 = jnp.float32
BF16 = jnp.bfloat16

D_MODEL = 2048
PLE_DIM = 256
ROPE_THETA = 10000.0
RMS_EPS = 1e-6
NEG = -1e30
LOG2E = math.log2(math.e)
LN2 = math.log(2.0)

DIFF_HEADS = 4
DIFF_QK_DIM = 64
DIFF_V_DIM = 128
DIFF_WIDTH = DIFF_HEADS * DIFF_V_DIM
DIL_PATTERNS = ((128, 1), (512, 4), (2048, 16))
DIL_HEADS = 4
DIL_HEAD_DIM = 128
DIL_WIDTH = DIL_HEADS * DIL_HEAD_DIM
DIL_RADIUS = 64
assert all(w // (2 * d) == DIL_RADIUS for w, d in DIL_PATTERNS)

LANES = 128
BF16_SUBLANES = 16
VT_ROWS = DIFF_V_DIM + BF16_SUBLANES
VMEM_LIMIT_BYTES = 56 * 1024 * 1024


def _cparams(semantics):
    return pltpu.CompilerParams(dimension_semantics=semantics, vmem_limit_bytes=VMEM_LIMIT_BYTES)


def _rope_tables(seq):
    pos = jnp.arange(seq, dtype=F32)[:, None]
    inv64 = ROPE_THETA ** (-jnp.arange(0, DIFF_QK_DIM, 2, dtype=F32) / DIFF_QK_DIM)
    ang = pos * inv64[None, :]
    c, s, z = jnp.cos(ang), jnp.sin(ang), jnp.zeros_like(ang)
    cos_a = jnp.concatenate([c, c, c, c], axis=-1)
    sin_a_up = jnp.concatenate([-s, z, -s, z], axis=-1)
    sin_a_dn = jnp.concatenate([z, s, z, s], axis=-1)
    inv128 = ROPE_THETA ** (-jnp.arange(0, DIL_HEAD_DIM, 2, dtype=F32) / DIL_HEAD_DIM)
    ang = pos * inv128[None, :]
    c, s = jnp.cos(ang), jnp.sin(ang)
    cos_b = jnp.concatenate([c, c], axis=-1)
    sin_b = jnp.concatenate([-s, s], axis=-1)
    return (cos_a, sin_a_up, sin_a_dn), (cos_b, sin_b)


def _rope64(x, cos, sin_up, sin_dn):
    return x * cos + pltpu.roll(x, 96, 1) * sin_up + pltpu.roll(x, 32, 1) * sin_dn


def _rope128(x, cos, sin):
    return x * cos + pltpu.roll(x, 64, 1) * sin


def _norm_matmul(x_ref, g_ref, w_ref):
    x = x_ref[...]
    ms = jnp.mean(x * x, axis=-1, keepdims=True)
    u = (x * lax.rsqrt(ms + RMS_EPS) * g_ref[...]).astype(BF16)
    return jnp.dot(u, w_ref[...], preferred_element_type=F32)


def _proj_diff_kernel(x_ref, g_ref, w_ref, cos_ref, sup_ref, sdn_ref, qt_ref, k_ref, vt_ref, gate_ref):
    z = _norm_matmul(x_ref, g_ref, w_ref)
    cos, sup, sdn = cos_ref[...], sup_ref[...], sdn_ref[...]
    qscale = DIFF_QK_DIM**-0.5 * LOG2E
    for h in range(DIFF_HEADS):
        lo = h * LANES
        q = _rope64(z[:, lo:lo + LANES], cos, sup, sdn) * qscale
        qt_ref[lo:lo + LANES, :] = q.T.astype(BF16)
        k = _rope64(z[:, DIFF_WIDTH + lo:DIFF_WIDTH + lo + LANES], cos, sup, sdn)
        k_ref[:, lo:lo + LANES] = k.astype(BF16)
        v = z[:, 2 * DIFF_WIDTH + lo:2 * DIFF_WIDTH + lo + LANES]
        vlo = h * VT_ROWS
        vt_ref[vlo:vlo + LANES, :] = v.T.astype(BF16)
        vt_ref[vlo + LANES:vlo + VT_ROWS, :] = jnp.ones((VT_ROWS - LANES, v.shape[0]), BF16)
    gate_ref[...] = z[:, 3 * DIFF_WIDTH:]


def _proj_dil_kernel(x_ref, g_ref, w_ref, cos_ref, sin_ref, *refs, dilation, with_gate):
    if with_gate:
        q_ref, k_ref, v_ref, gate_ref = refs[:4]
        refs = refs[4:]
    else:
        q_ref, k_ref, v_ref = refs[:3]
        refs = refs[3:]
    z = _norm_matmul(x_ref, g_ref, w_ref)
    cos, sin = cos_ref[...], sin_ref[...]
    qscale = DIL_HEAD_DIM**-0.5 * LOG2E
    tm = z.shape[0]
    rows = tm // dilation
    if dilation == 1:
        for h in range(DIL_HEADS):
            lo = h * LANES
            q_ref[0, :, lo:lo + LANES] = (_rope128(z[:, lo:lo + LANES], cos, sin) * qscale).astype(BF16)
            k_ref[0, :, lo:lo + LANES] = _rope128(
                z[:, DIL_WIDTH + lo:DIL_WIDTH + lo + LANES], cos, sin).astype(BF16)
        v_ref[0] = z[:, 2 * DIL_WIDTH:3 * DIL_WIDTH].astype(BF16)
    else:
        (stage_ref,) = refs
        for h in range(DIL_HEADS):
            lo = h * LANES
            stage_ref[h] = _rope128(z[:, lo:lo + LANES], cos, sin) * qscale
            stage_ref[DIL_HEADS + h] = _rope128(z[:, DIL_WIDTH + lo:DIL_WIDTH + lo + LANES], cos, sin)
            stage_ref[2 * DIL_HEADS + h] = z[:, 2 * DIL_WIDTH + lo:2 * DIL_WIDTH + lo + LANES]
        for r in range(dilation):
            for j, out_ref in enumerate((q_ref, k_ref, v_ref)):
                for h in range(DIL_HEADS):
                    blk = stage_ref[j * DIL_HEADS + h, pl.ds(r, rows, stride=dilation), :]
                    out_ref[r, :, h * LANES:(h + 1) * LANES] = blk.astype(BF16)
    if with_gate:
        gate_ref[...] = z[:, 3 * DIL_WIDTH:]


def _proj_tile(seq):
    return min(512, seq)


def _proj_diff(x, g, w, tabs):
    batch, seq, _ = x.shape
    tm = _proj_tile(seq)
    row = lambda b, i: (b, i, 0)
    col = lambda b, i: (b, 0, i)
    tab_spec = pl.BlockSpec((tm, LANES), lambda b, i: (i, 0))
    return pl.pallas_call(
        _proj_diff_kernel,
        grid=(batch, seq // tm),
        in_specs=[
            pl.BlockSpec((None, tm, D_MODEL), row),
            pl.BlockSpec((1, D_MODEL), lambda b, i: (0, 0)),
            pl.BlockSpec(w.shape, lambda b, i: (0, 0)),
            tab_spec, tab_spec, tab_spec,
        ],
        out_specs=[
            pl.BlockSpec((None, DIFF_WIDTH, tm), col),
            pl.BlockSpec((None, tm, DIFF_WIDTH), row),
            pl.BlockSpec((None, DIFF_HEADS * VT_ROWS, tm), col),
            pl.BlockSpec((None, tm, DIFF_WIDTH), row),
        ],
        out_shape=[
            jax.ShapeDtypeStruct((batch, DIFF_WIDTH, seq), BF16),
            jax.ShapeDtypeStruct((batch, seq, DIFF_WIDTH), BF16),
            jax.ShapeDtypeStruct((batch, DIFF_HEADS * VT_ROWS, seq), BF16),
            jax.ShapeDtypeStruct((batch, seq, DIFF_WIDTH), F32),
        ],
        compiler_params=_cparams(("parallel", "parallel")),
        name="proj_diff",
    )(x, g, w, *tabs)


def _proj_dil(x, g, w, tabs, dilation, with_gate):
    batch, seq, _ = x.shape
    tm = _proj_tile(seq)
    rows = tm // dilation
    sub = seq // dilation
    row = lambda b, i: (b, i, 0)
    res = lambda b, i: (b, 0, i, 0)
    tab_spec = pl.BlockSpec((tm, LANES), lambda b, i: (i, 0))
    res_spec = pl.BlockSpec((None, dilation, rows, DIL_WIDTH), res)
    res_shape = jax.ShapeDtypeStruct((batch, dilation, sub, DIL_WIDTH), BF16)
    out_specs = [res_spec, res_spec, res_spec]
    out_shape = [res_shape, res_shape, res_shape]
    if with_gate:
        out_specs.append(pl.BlockSpec((None, tm, DIL_WIDTH), row))
        out_shape.append(jax.ShapeDtypeStruct((batch, seq, DIL_WIDTH), F32))
    scratch = [] if dilation == 1 else [pltpu.VMEM((3 * DIL_HEADS, tm, LANES), F32)]
    return pl.pallas_call(
        functools.partial(_proj_dil_kernel, dilation=dilation, with_gate=with_gate),
        grid=(batch, seq // tm),
        in_specs=[
            pl.BlockSpec((None, tm, D_MODEL), row),
            pl.BlockSpec((1, D_MODEL), lambda b, i: (0, 0)),
            pl.BlockSpec(w.shape, lambda b, i: (0, 0)),
            tab_spec, tab_spec,
        ],
        out_specs=out_specs,
        out_shape=out_shape,
        scratch_shapes=scratch,
        compiler_params=_cparams(("parallel", "parallel")),
        name=f"proj_dil{dilation}",
    )(x, g, w, *tabs)


def _diff_attn_kernel(lam_ref, subln_ref, qt_ref, k_ref, vt_ref, gate_ref, o_ref,
                      w_scr, s_scr, bm_scr, m_scr, acc_scr, *, tk, lam_init):
    nk = k_ref.shape[0] // tk
    qt = qt_ref[...]
    row = lax.broadcasted_iota(jnp.int32, qt.shape, 0)
    zero = jnp.zeros_like(qt)
    w_scr[0] = jnp.where(row < DIFF_QK_DIM, qt, zero)
    w_scr[1] = jnp.where(row >= DIFF_QK_DIM, qt, zero)
    m_scr[...] = jnp.full(m_scr.shape, NEG, F32)
    acc_scr[...] = jnp.zeros(acc_scr.shape, F32)

    def scores(j, slot):
        start = pl.multiple_of(j * tk, tk)
        kb = k_ref[pl.ds(start, tk), :]
        for c in range(2):
            s = jnp.dot(kb, w_scr[c], preferred_element_type=F32)
            s_scr[slot, c] = s
            bm_scr[slot, c] = jnp.max(s, axis=0, keepdims=True)

    def accumulate(j, slot):
        start = pl.multiple_of(j * tk, tk)
        vb = vt_ref[:, pl.ds(start, tk)]
        for c in range(2):
            m_old = m_scr[c]
            m_new = jnp.maximum(m_old, bm_scr[slot, c])
            alpha = jnp.exp2(m_old - m_new)
            p = jnp.exp2(s_scr[slot, c] - m_new).astype(BF16)
            acc_scr[c] = alpha * acc_scr[c] + jnp.dot(vb, p, preferred_element_type=F32)
            m_scr[c] = m_new

    scores(0, 0)

    def body(i, carry):
        j = 2 * i
        scores(j + 1, 1)
        accumulate(j, 0)
        scores(jnp.minimum(j + 2, nk - 1), 0)
        accumulate(j + 1, 1)
        return carry

    lax.fori_loop(0, nk // 2, body, 0)

    lam_p = lam_ref[...]
    lam = (jnp.exp(jnp.sum(lam_p[0:1] * lam_p[1:2], axis=-1, keepdims=True))
           - jnp.exp(jnp.sum(lam_p[2:3] * lam_p[3:4], axis=-1, keepdims=True)) + lam_init)
    a0, a1 = acc_scr[0], acc_scr[1]
    o_t = (a0[:DIFF_V_DIM] * (1.0 / a0[DIFF_V_DIM:DIFF_V_DIM + 1])
           - lam * (a1[:DIFF_V_DIM] * (1.0 / a1[DIFF_V_DIM:DIFF_V_DIM + 1])))
    ms = jnp.mean(o_t * o_t, axis=0, keepdims=True)
    o = (o_t * lax.rsqrt(ms + RMS_EPS)).T
    g = gate_ref[...]
    silu = g / (1.0 + jnp.exp(-g))
    o_ref[...] = (o * (subln_ref[...] * (1.0 - lam_init)) * silu).astype(o_ref.dtype)


def _diff_attention(lam_params, subln, qt, k, vt, gate, lam_init):
    batch, seq, _ = k.shape
    tq = min(512, seq)
    tk = min(512, seq // 2)
    assert seq % (2 * tk) == 0
    return pl.pallas_call(
        functools.partial(_diff_attn_kernel, tk=tk, lam_init=lam_init),
        grid=(batch, DIFF_HEADS, seq // tq),
        in_specs=[
            pl.BlockSpec(lam_params.shape, lambda b, h, i: (0, 0)),
            pl.BlockSpec(subln.shape, lambda b, h, i: (0, 0)),
            pl.BlockSpec((None, LANES, tq), lambda b, h, i: (b, h, i)),
            pl.BlockSpec((None, seq, LANES), lambda b, h, i: (b, 0, h)),
            pl.BlockSpec((None, VT_ROWS, seq), lambda b, h, i: (b, h, 0)),
            pl.BlockSpec((None, tq, LANES), lambda b, h, i: (b, i, h)),
        ],
        out_specs=pl.BlockSpec((None, tq, LANES), lambda b, h, i: (b, i, h)),
        out_shape=jax.ShapeDtypeStruct((batch, seq, DIFF_WIDTH), BF16),
        scratch_shapes=[
            pltpu.VMEM((2, LANES, tq), BF16),
            pltpu.VMEM((2, 2, tk, tq), F32),
            pltpu.VMEM((2, 2, 1, tq), F32),
            pltpu.VMEM((2, 1, tq), F32),
            pltpu.VMEM((2, VT_ROWS, tq), F32),
        ],
        compiler_params=_cparams(("parallel", "parallel", "parallel")),
        name="diff_attn",
    )(lam_params, subln, qt, k, vt, gate)


def _band_attn_kernel(q_ref, kp_ref, kc_ref, kn_ref, vp_ref, vc_ref, vn_ref, o_ref, lse_ref, *, sub_len):
    tq = q_ref.shape[0]
    tw = tq + 2 * DIL_RADIUS
    i = pl.program_id(1)
    kwin = jnp.concatenate([kp_ref[...], kc_ref[...], kn_ref[...]], axis=0)
    vwin = jnp.concatenate([vp_ref[...], vc_ref[...], vn_ref[...]], axis=0)
    row = lax.broadcasted_iota(jnp.int32, (tq, tw), 0)
    col = lax.broadcasted_iota(jnp.int32, (tq, tw), 1)
    rel = col - row
    kpos = i * tq - DIL_RADIUS + col
    bias = jnp.where(rel >= 0, 0.0, NEG)
    bias = jnp.where(rel <= 2 * DIL_RADIUS, bias, NEG)
    bias = jnp.where(kpos >= 0, bias, NEG)
    bias = jnp.where(kpos < sub_len, bias, NEG).astype(F32)
    for h in range(DIL_HEADS):
        lo = h * LANES
        s = lax.dot_general(q_ref[:, lo:lo + LANES], kwin[:, lo:lo + LANES],
                            (((1,), (1,)), ((), ())), preferred_element_type=F32)
        s = s + bias
        m = jnp.max(s, axis=-1, keepdims=True)
        p = jnp.exp2(s - m)
        den = jnp.sum(p, axis=-1, keepdims=True)
        o = jnp.dot(p.astype(BF16), vwin[:, lo:lo + LANES], preferred_element_type=F32)
        o_ref[:, lo:lo + LANES] = o * (1.0 / den)
        lse = (m + jnp.log2(den)) * LN2
        lse_ref[:, lo:lo + LANES] = jnp.broadcast_to(lse, (tq, LANES))


def _band_attention(q, k, v):
    nseq, sub, _ = q.shape
    tq = min(512, sub)
    per = tq // DIL_RADIUS
    nhalo = sub // DIL_RADIUS
    cur = pl.BlockSpec((None, tq, DIL_WIDTH), lambda n, i: (n, i, 0))
    prev = pl.BlockSpec((None, DIL_RADIUS, DIL_WIDTH), lambda n, i: (n, jnp.maximum(i * per - 1, 0), 0))
    nxt = pl.BlockSpec((None, DIL_RADIUS, DIL_WIDTH),
                       lambda n, i: (n, jnp.minimum((i + 1) * per, nhalo - 1), 0))
    shape = jax.ShapeDtypeStruct((nseq, sub, DIL_WIDTH), F32)
    return pl.pallas_call(
        functools.partial(_band_attn_kernel, sub_len=sub),
        grid=(nseq, sub // tq),
        in_specs=[cur, prev, cur, nxt, prev, cur, nxt],
        out_specs=[cur, cur],
        out_shape=[shape, shape],
        compiler_params=_cparams(("parallel", "parallel")),
        name=f"band_attn{sub}",
    )(q, k, k, k, v, v, v)


def _rms(x, g):
    ms = jnp.mean(x * x, axis=-1, keepdims=True)
    return x * lax.rsqrt(ms + RMS_EPS) * g


def _tail_kernel(x_ref, ya_ref, o0_ref, l0_ref, o1_ref, l1_ref, o2_ref, l2_ref, bg_ref, p_ref,
                 wout_ref, wgate_ref, wproj_ref, plen_ref, fnorm_ref, y_ref,
                 o1_scr, l1_scr, o2_scr, l2_scr, yb_scr):
    tm = x_ref.shape[0]
    for src, dst in ((o1_ref, o1_scr), (l1_ref, l1_scr), (o2_ref, o2_scr), (l2_ref, l2_scr)):
        d = src.shape[0]
        for r in range(d):
            for h in range(DIL_HEADS):
                dst[h, pl.ds(r, tm // d, stride=d), :] = src[r, :, h * LANES:(h + 1) * LANES]
    for h in range(DIL_HEADS):
        slab = slice(h * LANES, (h + 1) * LANES)
        l0, l1, l2 = l0_ref[:, slab], l1_scr[h], l2_scr[h]
        mx = jnp.maximum(jnp.maximum(l0, l1), l2)
        e0, e1, e2 = jnp.exp(l0 - mx), jnp.exp(l1 - mx), jnp.exp(l2 - mx)
        ob = (e0 * o0_ref[:, slab] + e1 * o1_scr[h] + e2 * o2_scr[h]) * (1.0 / (e0 + e1 + e2))
        bg = bg_ref[:, slab]
        yb_scr[:, slab] = (ob * (bg / (1.0 + jnp.exp(-bg)))).astype(BF16)
    h = (x_ref[...]
         + jnp.dot(ya_ref[...], wout_ref[:DIFF_WIDTH, :], preferred_element_type=F32)
         + jnp.dot(yb_scr[...], wout_ref[DIFF_WIDTH:, :], preferred_element_type=F32))
    n = _rms(h, plen_ref[...]).astype(BF16)
    gate = 1.0 / (1.0 + jnp.exp(-jnp.dot(n, wgate_ref[...], preferred_element_type=F32)))
    h = h + gate * jnp.dot(p_ref[...].astype(BF16), wproj_ref[...], preferred_element_type=F32)
    y_ref[...] = _rms(h, fnorm_ref[...])


def _tail(x, ya, o0, l0, o1, l1, o2, l2, bg, p, wout, wgate, wproj, plen, fnorm):
    batch, seq, _ = x.shape
    tm = min(256, seq)
    row = lambda b, i: (b, i, 0)
    full = lambda a: pl.BlockSpec(a.shape, lambda b, i: (0,) * a.ndim)
    wide = pl.BlockSpec((None, tm, D_MODEL), row)
    half = pl.BlockSpec((None, tm, DIL_WIDTH), row)

    def res(a):
        d = a.shape[1]
        return pl.BlockSpec((None, d, tm // d, DIL_WIDTH), lambda b, i: (b, 0, i, 0))

    return pl.pallas_call(
        _tail_kernel,
        grid=(batch, seq // tm),
        in_specs=[wide, half, half, half, res(o1), res(l1), res(o2), res(l2), half,
                  pl.BlockSpec((None, tm, PLE_DIM), row),
                  full(wout), full(wgate), full(wproj), full(plen), full(fnorm)],
        out_specs=wide,
        out_shape=jax.ShapeDtypeStruct((batch, seq, D_MODEL), F32),
        scratch_shapes=[pltpu.VMEM((DIL_HEADS, tm, LANES), F32) for _ in range(4)]
        + [pltpu.VMEM((tm, DIL_WIDTH), BF16)],
        compiler_params=_cparams(("parallel", "parallel")),
        name="tail",
    )(x, ya, o0, l0, o1, l1, o2, l2, bg, p, wout, wgate, wproj, plen, fnorm)


def _split_w_in(w_in):
    w = w_in.astype(BF16)
    a_end = 4 * DIFF_WIDTH
    groups = len(DIL_PATTERNS)
    bq, bk, bv = (a_end + j * groups * DIL_WIDTH for j in range(3))
    bg = a_end + 3 * groups * DIL_WIDTH
    w_diff = w[:, :a_end]
    w_dil = []
    for g in range(groups):
        cols = [w[:, base + g * DIL_WIDTH: base + (g + 1) * DIL_WIDTH] for base in (bq, bk, bv)]
        if g == 0:
            cols.append(w[:, bg:bg + DIL_WIDTH])
        w_dil.append(jnp.concatenate(cols, axis=1))
    return w_diff, w_dil


def _layer(x, p, layer_idx, norm_mix, w_diff, w_dil, lam_params, subln, wout, wgate, wproj, plen, fnorm):
    batch, seq, _ = x.shape
    tabs_a, tabs_b = _rope_tables(seq)
    lam_init = 0.8 - 0.6 * math.exp(-0.3 * layer_idx)

    qt, k, vt, ag = _proj_diff(x, norm_mix, w_diff, tabs_a)
    ya = _diff_attention(lam_params, subln, qt, k, vt, ag, lam_init)

    outs, lses, bg = [], [], None
    for g, (_, dilation) in enumerate(DIL_PATTERNS):
        res = _proj_dil(x, norm_mix, w_dil[g], tabs_b, dilation, with_gate=(g == 0))
        if g == 0:
            bg = res[3]
        sub = seq // dilation
        q, kk, v = (a.reshape(batch * dilation, sub, DIL_WIDTH) for a in res[:3])
        o, lse = _band_attention(q, kk, v)
        outs.append(o.reshape(batch, dilation, sub, DIL_WIDTH))
        lses.append(lse.reshape(batch, dilation, sub, DIL_WIDTH))

    return _tail(x, ya, outs[0].reshape(batch, seq, DIL_WIDTH), lses[0].reshape(batch, seq, DIL_WIDTH),
                 outs[1], lses[1], outs[2], lses[2], bg, p, wout, wgate, wproj, plen, fnorm)


def kernel(x_prompt, x_sample, p_prompt, p_sample, norm_mix, w_in, lam_q1, lam_k1, lam_q2, lam_k2,
           subln, w_out, ple_norm, w_ple_gate, w_ple_proj, final_norm):
    depth = w_in.shape[0]
    assert depth == 1, "the fused tail applies the final norm, so it handles a single layer"
    i = 0
    w_diff, w_dil = _split_w_in(w_in[i])
    lam_params = jnp.stack([lam_q1[i], lam_k1[i], lam_q2[i], lam_k2[i]], axis=0)
    args = (norm_mix[i][None, :], w_diff, w_dil, lam_params, subln[i][None, :],
            w_out[i].astype(BF16), w_ple_gate[i].astype(BF16), w_ple_proj[i].astype(BF16),
            ple_norm[i][None, :], final_norm[None, :])
    y_prompt = _layer(x_prompt, p_prompt[i], i, *args)
    y_sample = _layer(x_sample, p_sample[i], i, *args)
    return (y_prompt, y_sample)
```

```python
import functools
import math

import jax
import jax.numpy as jnp
from jax import lax
from jax.experimental import pallas as pl
from jax.experimental.pallas import tpu as pltpu

F32 = jnp.float32
BF16 = jnp.bfloat16

D_MODEL = 2048
PLE_DIM = 256
ROPE_THETA = 10000.0
RMS_EPS = 1e-6
NEG = -1e30
LOG2E = math.log2(math.e)
LN2 = math.log(2.0)

DIFF_HEADS = 4
DIFF_QK_DIM = 64
DIFF_V_DIM = 128
DIFF_WIDTH = DIFF_HEADS * DIFF_V_DIM
DIL_PATTERNS = ((128, 1), (512, 4), (2048, 16))
DIL_HEADS = 4
DIL_HEAD_DIM = 128
DIL_WIDTH = DIL_HEADS * DIL_HEAD_DIM
DIL_RADIUS = 64
BAND_UNIT = 128
assert all(w // (2 * d) == DIL_RADIUS for w, d in DIL_PATTERNS)

LANES = 128
BF16_SUBLANES = 16
VT_ROWS = DIFF_V_DIM + BF16_SUBLANES
VMEM_LIMIT_BYTES = 56 * 1024 * 1024


def _cparams(semantics):
    return pltpu.CompilerParams(dimension_semantics=semantics, vmem_limit_bytes=VMEM_LIMIT_BYTES)


def _rope_tables(seq):
    pos = jnp.arange(seq, dtype=F32)[:, None]
    inv64 = ROPE_THETA ** (-jnp.arange(0, DIFF_QK_DIM, 2, dtype=F32) / DIFF_QK_DIM)
    ang = pos * inv64[None, :]
    c, s, z = jnp.cos(ang), jnp.sin(ang), jnp.zeros_like(ang)
    cos_a = jnp.concatenate([c, c, c, c], axis=-1)
    sin_a_up = jnp.concatenate([-s, z, -s, z], axis=-1)
    sin_a_dn = jnp.concatenate([z, s, z, s], axis=-1)
    inv128 = ROPE_THETA ** (-jnp.arange(0, DIL_HEAD_DIM, 2, dtype=F32) / DIL_HEAD_DIM)
    ang = pos * inv128[None, :]
    c, s = jnp.cos(ang), jnp.sin(ang)
    cos_b = jnp.concatenate([c, c], axis=-1)
    sin_b = jnp.concatenate([-s, s], axis=-1)
    return (cos_a, sin_a_up, sin_a_dn), (cos_b, sin_b)


def _rope64(x, cos, sin_up, sin_dn):
    return x * cos + pltpu.roll(x, 96, 1) * sin_up + pltpu.roll(x, 32, 1) * sin_dn


def _rope128(x, cos, sin):
    return x * cos + pltpu.roll(x, 64, 1) * sin


def _norm_matmul(x_ref, g_ref, w_ref):
    x = x_ref[...]
    ms = jnp.mean(x * x, axis=-1, keepdims=True)
    u = (x * lax.rsqrt(ms + RMS_EPS) * g_ref[...]).astype(BF16)
    return jnp.dot(u, w_ref[...], preferred_element_type=F32)


def _proj_diff_kernel(x_ref, g_ref, w_ref, cos_ref, sup_ref, sdn_ref, qt_ref, k_ref, vt_ref, gate_ref):
    z = _norm_matmul(x_ref, g_ref, w_ref)
    cos, sup, sdn = cos_ref[...], sup_ref[...], sdn_ref[...]
    qscale = DIFF_QK_DIM**-0.5 * LOG2E
    for h in range(DIFF_HEADS):
        lo = h * LANES
        q = _rope64(z[:, lo:lo + LANES], cos, sup, sdn) * qscale
        qt_ref[lo:lo + LANES, :] = q.T.astype(BF16)
        k = _rope64(z[:, DIFF_WIDTH + lo:DIFF_WIDTH + lo + LANES], cos, sup, sdn)
        k_ref[:, lo:lo + LANES] = k.astype(BF16)
        v = z[:, 2 * DIFF_WIDTH + lo:2 * DIFF_WIDTH + lo + LANES]
        vlo = h * VT_ROWS
        vt_ref[vlo:vlo + LANES, :] = v.T.astype(BF16)
        vt_ref[vlo + LANES:vlo + VT_ROWS, :] = jnp.ones((VT_ROWS - LANES, v.shape[0]), BF16)
    gate_ref[...] = z[:, 3 * DIFF_WIDTH:]


def _proj_dil_kernel(x_ref, g_ref, w_ref, cos_ref, sin_ref, *refs, dilation, with_gate):
    if with_gate:
        q_ref, k_ref, v_ref, gate_ref = refs[:4]
        refs = refs[4:]
    else:
        q_ref, k_ref, v_ref = refs[:3]
        refs = refs[3:]
    z = _norm_matmul(x_ref, g_ref, w_ref)
    cos, sin = cos_ref[...], sin_ref[...]
    qscale = DIL_HEAD_DIM**-0.5 * LOG2E
    tm = z.shape[0]
    rows = tm // dilation
    if dilation == 1:
        for h in range(DIL_HEADS):
            lo = h * LANES
            q_ref[0, :, lo:lo + LANES] = (_rope128(z[:, lo:lo + LANES], cos, sin) * qscale).astype(BF16)
            k_ref[0, :, lo:lo + LANES] = _rope128(
                z[:, DIL_WIDTH + lo:DIL_WIDTH + lo + LANES], cos, sin).astype(BF16)
        v_ref[0] = z[:, 2 * DIL_WIDTH:3 * DIL_WIDTH].astype(BF16)
    else:
        (stage_ref,) = refs
        for h in range(DIL_HEADS):
            lo = h * LANES
            stage_ref[h] = _rope128(z[:, lo:lo + LANES], cos, sin) * qscale
            stage_ref[DIL_HEADS + h] = _rope128(z[:, DIL_WIDTH + lo:DIL_WIDTH + lo + LANES], cos, sin)
            stage_ref[2 * DIL_HEADS + h] = z[:, 2 * DIL_WIDTH + lo:2 * DIL_WIDTH + lo + LANES]
        for r in range(dilation):
            for j, out_ref in enumerate((q_ref, k_ref, v_ref)):
                for h in range(DIL_HEADS):
                    blk = stage_ref[j * DIL_HEADS + h, pl.ds(r, rows, stride=dilation), :]
                    out_ref[r, :, h * LANES:(h + 1) * LANES] = blk.astype(BF16)
    if with_gate:
        gate_ref[...] = z[:, 3 * DIL_WIDTH:]


def _proj_tile(seq):
    return min(512, seq)


def _proj_diff(x, g, w, tabs):
    batch, seq, _ = x.shape
    tm = _proj_tile(seq)
    row = lambda b, i: (b, i, 0)
    col = lambda b, i: (b, 0, i)
    tab_spec = pl.BlockSpec((tm, LANES), lambda b, i: (i, 0))
    return pl.pallas_call(
        _proj_diff_kernel,
        grid=(batch, seq // tm),
        in_specs=[
            pl.BlockSpec((None, tm, D_MODEL), row),
            pl.BlockSpec((1, D_MODEL), lambda b, i: (0, 0)),
            pl.BlockSpec(w.shape, lambda b, i: (0, 0)),
            tab_spec, tab_spec, tab_spec,
        ],
        out_specs=[
            pl.BlockSpec((None, DIFF_WIDTH, tm), col),
            pl.BlockSpec((None, tm, DIFF_WIDTH), row),
            pl.BlockSpec((None, DIFF_HEADS * VT_ROWS, tm), col),
            pl.BlockSpec((None, tm, DIFF_WIDTH), row),
        ],
        out_shape=[
            jax.ShapeDtypeStruct((batch, DIFF_WIDTH, seq), BF16),
            jax.ShapeDtypeStruct((batch, seq, DIFF_WIDTH), BF16),
            jax.ShapeDtypeStruct((batch, DIFF_HEADS * VT_ROWS, seq), BF16),
            jax.ShapeDtypeStruct((batch, seq, DIFF_WIDTH), F32),
        ],
        compiler_params=_cparams(("parallel", "parallel")),
        name="proj_diff",
    )(x, g, w, *tabs)


def _proj_dil(x, g, w, tabs, dilation, with_gate):
    batch, seq, _ = x.shape
    tm = _proj_tile(seq)
    rows = tm // dilation
    sub = seq // dilation
    row = lambda b, i: (b, i, 0)
    res = lambda b, i: (b, 0, i, 0)
    tab_spec = pl.BlockSpec((tm, LANES), lambda b, i: (i, 0))
    res_spec = pl.BlockSpec((None, dilation, rows, DIL_WIDTH), res)
    res_shape = jax.ShapeDtypeStruct((batch, dilation, sub, DIL_WIDTH), BF16)
    out_specs = [res_spec, res_spec, res_spec]
    out_shape = [res_shape, res_shape, res_shape]
    if with_gate:
        out_specs.append(pl.BlockSpec((None, tm, DIL_WIDTH), row))
        out_shape.append(jax.ShapeDtypeStruct((batch, seq, DIL_WIDTH), F32))
    scratch = [] if dilation == 1 else [pltpu.VMEM((3 * DIL_HEADS, tm, LANES), F32)]
    return pl.pallas_call(
        functools.partial(_proj_dil_kernel, dilation=dilation, with_gate=with_gate),
        grid=(batch, seq // tm),
        in_specs=[
            pl.BlockSpec((None, tm, D_MODEL), row),
            pl.BlockSpec((1, D_MODEL), lambda b, i: (0, 0)),
            pl.BlockSpec(w.shape, lambda b, i: (0, 0)),
            tab_spec, tab_spec,
        ],
        out_specs=out_specs,
        out_shape=out_shape,
        scratch_shapes=scratch,
        compiler_params=_cparams(("parallel", "parallel")),
        name=f"proj_dil{dilation}",
    )(x, g, w, *tabs)


def _diff_attn_kernel(lam_ref, subln_ref, qt_ref, k_ref, vt_ref, gate_ref, o_ref,
                      w_scr, s_scr, bm_scr, m_scr, acc_scr, *, tk, unroll, lam_init):
    nk = k_ref.shape[0] // tk
    qt = qt_ref[...]
    row = lax.broadcasted_iota(jnp.int32, qt.shape, 0)
    zero = jnp.zeros_like(qt)
    w_scr[0] = jnp.where(row < DIFF_QK_DIM, qt, zero)
    w_scr[1] = jnp.where(row >= DIFF_QK_DIM, qt, zero)
    m_scr[...] = jnp.full(m_scr.shape, NEG, F32)
    acc_scr[...] = jnp.zeros(acc_scr.shape, F32)

    def scores(j, slot):
        start = pl.multiple_of(j * tk, tk)
        kb = k_ref[pl.ds(start, tk), :]
        for c in range(2):
            s = jnp.dot(kb, w_scr[c], preferred_element_type=F32)
            s_scr[slot, c] = s
            bm_scr[slot, c] = jnp.max(s, axis=0, keepdims=True)

    def accumulate(j, slot):
        start = pl.multiple_of(j * tk, tk)
        vb = vt_ref[:, pl.ds(start, tk)]
        for c in range(2):
            m_old = m_scr[c]
            m_new = jnp.maximum(m_old, bm_scr[slot, c])
            alpha = jnp.exp2(m_old - m_new)
            p = jnp.exp2(s_scr[slot, c] - m_new).astype(BF16)
            acc_scr[c] = alpha * acc_scr[c] + jnp.dot(vb, p, preferred_element_type=F32)
            m_scr[c] = m_new

    scores(0, 0)

    def body(i, carry):
        j = unroll * i
        for u in range(unroll):
            scores(jnp.minimum(j + u + 1, nk - 1), (u + 1) % 2)
            accumulate(j + u, u % 2)
        return carry

    lax.fori_loop(0, nk // unroll, body, 0)

    lam_p = lam_ref[...]
    lam = (jnp.exp(jnp.sum(lam_p[0:1] * lam_p[1:2], axis=-1, keepdims=True))
           - jnp.exp(jnp.sum(lam_p[2:3] * lam_p[3:4], axis=-1, keepdims=True)) + lam_init)
    a0, a1 = acc_scr[0], acc_scr[1]
    o_t = (a0[:DIFF_V_DIM] * (1.0 / a0[DIFF_V_DIM:DIFF_V_DIM + 1])
           - lam * (a1[:DIFF_V_DIM] * (1.0 / a1[DIFF_V_DIM:DIFF_V_DIM + 1])))
    ms = jnp.mean(o_t * o_t, axis=0, keepdims=True)
    o = (o_t * lax.rsqrt(ms + RMS_EPS)).T
    g = gate_ref[...]
    silu = g / (1.0 + jnp.exp(-g))
    o_ref[...] = (o * (subln_ref[...] * (1.0 - lam_init)) * silu).astype(o_ref.dtype)


def _diff_attention(lam_params, subln, qt, k, vt, gate, lam_init):
    batch, seq, _ = k.shape
    tq = min(512, seq)
    tk = min(512, seq // 2)
    unroll = 8 if seq % (8 * tk) == 0 else 2
    assert seq % (unroll * tk) == 0
    return pl.pallas_call(
        functools.partial(_diff_attn_kernel, tk=tk, unroll=unroll, lam_init=lam_init),
        grid=(batch, DIFF_HEADS, seq // tq),
        in_specs=[
            pl.BlockSpec(lam_params.shape, lambda b, h, i: (0, 0)),
            pl.BlockSpec(subln.shape, lambda b, h, i: (0, 0)),
            pl.BlockSpec((None, LANES, tq), lambda b, h, i: (b, h, i)),
            pl.BlockSpec((None, seq, LANES), lambda b, h, i: (b, 0, h)),
            pl.BlockSpec((None, VT_ROWS, seq), lambda b, h, i: (b, h, 0)),
            pl.BlockSpec((None, tq, LANES), lambda b, h, i: (b, i, h)),
        ],
        out_specs=pl.BlockSpec((None, tq, LANES), lambda b, h, i: (b, i, h)),
        out_shape=jax.ShapeDtypeStruct((batch, seq, DIFF_WIDTH), BF16),
        scratch_shapes=[
            pltpu.VMEM((2, LANES, tq), BF16),
            pltpu.VMEM((2, 2, tk, tq), F32),
            pltpu.VMEM((2, 2, 1, tq), F32),
            pltpu.VMEM((2, 1, tq), F32),
            pltpu.VMEM((2, VT_ROWS, tq), F32),
        ],
        compiler_params=_cparams(("parallel", "parallel", "parallel")),
        name="diff_attn",
    )(lam_params, subln, qt, k, vt, gate)


def _band_window(prev_ref, cur_ref, next_ref, unit, n_units, lo):
    r, q = DIL_RADIUS, BAND_UNIT
    lanes = slice(lo, lo + LANES)
    pieces = []
    if unit == 0:
        pieces.append(prev_ref[:, lanes])
    first = max(unit * q - r, 0)
    last = min(unit * q + q + r, n_units * q)
    pieces.append(cur_ref[first:last, lanes])
    if unit == n_units - 1:
        pieces.append(next_ref[:, lanes])
    return pieces[0] if len(pieces) == 1 else jnp.concatenate(pieces, axis=0)


def _band_attn_kernel(q_ref, kp_ref, kc_ref, kn_ref, vp_ref, vc_ref, vn_ref, o_ref, lse_ref, *, sub_len):
    tq = q_ref.shape[0]
    n_units = tq // BAND_UNIT
    tw = BAND_UNIT + 2 * DIL_RADIUS
    i = pl.program_id(1)
    row = lax.broadcasted_iota(jnp.int32, (BAND_UNIT, tw), 0)
    col = lax.broadcasted_iota(jnp.int32, (BAND_UNIT, tw), 1)
    rel = col - row
    band = jnp.where(rel >= 0, 0.0, NEG)
    band = jnp.where(rel <= 2 * DIL_RADIUS, band, NEG).astype(F32)
    for u in range(n_units):
        bias = band
        kpos = i * tq + (u * BAND_UNIT - DIL_RADIUS) + col
        if u == 0:
            bias = jnp.where(kpos >= 0, bias, NEG)
        if u == n_units - 1:
            bias = jnp.where(kpos < sub_len, bias, NEG)
        rows = slice(u * BAND_UNIT, (u + 1) * BAND_UNIT)
        for h in range(DIL_HEADS):
            lo = h * LANES
            kwin = _band_window(kp_ref, kc_ref, kn_ref, u, n_units, lo)
            vwin = _band_window(vp_ref, vc_ref, vn_ref, u, n_units, lo)
            s = lax.dot_general(q_ref[rows, lo:lo + LANES], kwin,
                                (((1,), (1,)), ((), ())), preferred_element_type=F32)
            s = s + bias
            m = jnp.max(s, axis=-1, keepdims=True)
            p = jnp.exp2(s - m)
            den = jnp.sum(p, axis=-1, keepdims=True)
            o = jnp.dot(p.astype(BF16), vwin, preferred_element_type=F32)
            o_ref[rows, lo:lo + LANES] = o * (1.0 / den)
            lse = (m + jnp.log2(den)) * LN2
            lse_ref[rows, lo:lo + LANES] = jnp.broadcast_to(lse, (BAND_UNIT, LANES))


def _band_attention(q, k, v):
    nseq, sub, _ = q.shape
    tq = min(512, sub)
    per = tq // DIL_RADIUS
    nhalo = sub // DIL_RADIUS
    cur = pl.BlockSpec((None, tq, DIL_WIDTH), lambda n, i: (n, i, 0))
    prev = pl.BlockSpec((None, DIL_RADIUS, DIL_WIDTH), lambda n, i: (n, jnp.maximum(i * per - 1, 0), 0))
    nxt = pl.BlockSpec((None, DIL_RADIUS, DIL_WIDTH),
                       lambda n, i: (n, jnp.minimum((i + 1) * per, nhalo - 1), 0))
    shape = jax.ShapeDtypeStruct((nseq, sub, DIL_WIDTH), F32)
    return pl.pallas_call(
        functools.partial(_band_attn_kernel, sub_len=sub),
        grid=(nseq, sub // tq),
        in_specs=[cur, prev, cur, nxt, prev, cur, nxt],
        out_specs=[cur, cur],
        out_shape=[shape, shape],
        compiler_params=_cparams(("parallel", "parallel")),
        name=f"band_attn{sub}",
    )(q, k, k, k, v, v, v)


def _rms(x, g):
    ms = jnp.mean(x * x, axis=-1, keepdims=True)
    return x * lax.rsqrt(ms + RMS_EPS) * g


def _tail_kernel(x_ref, ya_ref, o0_ref, l0_ref, o1_ref, l1_ref, o2_ref, l2_ref, bg_ref, p_ref,
                 wout_ref, wgate_ref, wproj_ref, plen_ref, fnorm_ref, y_ref,
                 o1_scr, l1_scr, o2_scr, l2_scr, yb_scr):
    tm = x_ref.shape[0]
    for src, dst in ((o1_ref, o1_scr), (l1_ref, l1_scr), (o2_ref, o2_scr), (l2_ref, l2_scr)):
        d = src.shape[0]
        for r in range(d):
            for h in range(DIL_HEADS):
                dst[h, pl.ds(r, tm // d, stride=d), :] = src[r, :, h * LANES:(h + 1) * LANES]
    for h in range(DIL_HEADS):
        slab = slice(h * LANES, (h + 1) * LANES)
        l0, l1, l2 = l0_ref[:, slab], l1_scr[h], l2_scr[h]
        mx = jnp.maximum(jnp.maximum(l0, l1), l2)
        e0, e1, e2 = jnp.exp(l0 - mx), jnp.exp(l1 - mx), jnp.exp(l2 - mx)
        ob = (e0 * o0_ref[:, slab] + e1 * o1_scr[h] + e2 * o2_scr[h]) * (1.0 / (e0 + e1 + e2))
        bg = bg_ref[:, slab]
        yb_scr[:, slab] = (ob * (bg / (1.0 + jnp.exp(-bg)))).astype(BF16)
    h = (x_ref[...]
         + jnp.dot(ya_ref[...], wout_ref[:DIFF_WIDTH, :], preferred_element_type=F32)
         + jnp.dot(yb_scr[...], wout_ref[DIFF_WIDTH:, :], preferred_element_type=F32))
    n = _rms(h, plen_ref[...]).astype(BF16)
    gate = 1.0 / (1.0 + jnp.exp(-jnp.dot(n, wgate_ref[...], preferred_element_type=F32)))
    h = h + gate * jnp.dot(p_ref[...].astype(BF16), wproj_ref[...], preferred_element_type=F32)
    y_ref[...] = _rms(h, fnorm_ref[...])


def _tail(x, ya, o0, l0, o1, l1, o2, l2, bg, p, wout, wgate, wproj, plen, fnorm):
    batch, seq, _ = x.shape
    tm = min(256, seq)
    row = lambda b, i: (b, i, 0)
    full = lambda a: pl.BlockSpec(a.shape, lambda b, i: (0,) * a.ndim)
    wide = pl.BlockSpec((None, tm, D_MODEL), row)
    half = pl.BlockSpec((None, tm, DIL_WIDTH), row)

    def res(a):
        d = a.shape[1]
        return pl.BlockSpec((None, d, tm // d, DIL_WIDTH), lambda b, i: (b, 0, i, 0))

    return pl.pallas_call(
        _tail_kernel,
        grid=(batch, seq // tm),
        in_specs=[wide, half, half, half, res(o1), res(l1), res(o2), res(l2), half,
                  pl.BlockSpec((None, tm, PLE_DIM), row),
                  full(wout), full(wgate), full(wproj), full(plen), full(fnorm)],
        out_specs=wide,
        out_shape=jax.ShapeDtypeStruct((batch, seq, D_MODEL), F32),
        scratch_shapes=[pltpu.VMEM((DIL_HEADS, tm, LANES), F32) for _ in range(4)]
        + [pltpu.VMEM((tm, DIL_WIDTH), BF16)],
        compiler_params=_cparams(("parallel", "parallel")),
        name="tail",
    )(x, ya, o0, l0, o1, l1, o2, l2, bg, p, wout, wgate, wproj, plen, fnorm)


def _split_w_in(w_in):
    w = w_in.astype(BF16)
    a_end = 4 * DIFF_WIDTH
    groups = len(DIL_PATTERNS)
    bq, bk, bv = (a_end + j * groups * DIL_WIDTH for j in range(3))
    bg = a_end + 3 * groups * DIL_WIDTH
    w_diff = w[:, :a_end]
    w_dil = []
    for g in range(groups):
        cols = [w[:, base + g * DIL_WIDTH: base + (g + 1) * DIL_WIDTH] for base in (bq, bk, bv)]
        if g == 0:
            cols.append(w[:, bg:bg + DIL_WIDTH])
        w_dil.append(jnp.concatenate(cols, axis=1))
    return w_diff, w_dil


def _layer(x, p, layer_idx, norm_mix, w_diff, w_dil, lam_params, subln, wout, wgate, wproj, plen, fnorm):
    batch, seq, _ = x.shape
    tabs_a, tabs_b = _rope_tables(seq)
    lam_init = 0.8 - 0.6 * math.exp(-0.3 * layer_idx)

    qt, k, vt, ag = _proj_diff(x, norm_mix, w_diff, tabs_a)
    ya = _diff_attention(lam_params, subln, qt, k, vt, ag, lam_init)

    outs, lses, bg = [], [], None
    for g, (_, dilation) in enumerate(DIL_PATTERNS):
        res = _proj_dil(x, norm_mix, w_dil[g], tabs_b, dilation, with_gate=(g == 0))
        if g == 0:
            bg = res[3]
        sub = seq // dilation
        q, kk, v = (a.reshape(batch * dilation, sub, DIL_WIDTH) for a in res[:3])
        o, lse = _band_attention(q, kk, v)
        outs.append(o.reshape(batch, dilation, sub, DIL_WIDTH))
        lses.append(lse.reshape(batch, dilation, sub, DIL_WIDTH))

    return _tail(x, ya, outs[0].reshape(batch, seq, DIL_WIDTH), lses[0].reshape(batch, seq, DIL_WIDTH),
                 outs[1], lses[1], outs[2], lses[2], bg, p, wout, wgate, wproj, plen, fnorm)


def kernel(x_prompt, x_sample, p_prompt, p_sample, norm_mix, w_in, lam_q1, lam_k1, lam_q2, lam_k2,
           subln, w_out, ple_norm, w_ple_gate, w_ple_proj, final_norm):
    depth = w_in.shape[0]
    assert depth == 1, "the fused tail applies the final norm, so it handles a single layer"
    i = 0
    w_diff, w_dil = _split_w_in(w_in[i])
    lam_params = jnp.stack([lam_q1[i], lam_k1[i], lam_q2[i], lam_k2[i]], axis=0)
    args = (norm_mix[i][None, :], w_diff, w_dil, lam_params, subln[i][None, :],
            w_out[i].astype(BF16), w_ple_gate[i].astype(BF16), w_ple_proj[i].astype(BF16),
            ple_norm[i][None, :], final_norm[None, :])
    y_prompt = _layer(x_prompt, p_prompt[i], i, *args)
    y_sample = _layer(x_sample, p_sample[i], i, *args)
    return (y_prompt, y_sample)
```

```python
import functools
import math

import jax
import jax.numpy as jnp
from jax import lax
from jax.experimental import pallas as pl
from jax.experimental.pallas import tpu as pltpu

F32 = jnp.float32
BF16 = jnp.bfloat16

D_MODEL = 2048
PLE_DIM = 256
ROPE_THETA = 10000.0
RMS_EPS = 1e-6
NEG = -1e30
LOG2E = math.log2(math.e)
LN2 = math.log(2.0)

DIFF_HEADS = 4
DIFF_QK_DIM = 64
DIFF_V_DIM = 128
DIFF_WIDTH = DIFF_HEADS * DIFF_V_DIM
DIL_PATTERNS = ((128, 1), (512, 4), (2048, 16))
DIL_HEADS = 4
DIL_HEAD_DIM = 128
DIL_WIDTH = DIL_HEADS * DIL_HEAD_DIM
DIL_RADIUS = 64
BAND_UNIT = 128
assert all(w // (2 * d) == DIL_RADIUS for w, d in DIL_PATTERNS)

LANES = 128
MXU_DIM = 256
LSE_LANES = LANES // DIL_HEADS
BF16_SUBLANES = 16
VT_ROWS = DIFF_V_DIM + BF16_SUBLANES
VMEM_LIMIT_BYTES = 56 * 1024 * 1024


def _cparams(semantics):
    return pltpu.CompilerParams(dimension_semantics=semantics, vmem_limit_bytes=VMEM_LIMIT_BYTES)


def _rope_tables(seq):
    pos = jnp.arange(seq, dtype=F32)[:, None]
    inv64 = ROPE_THETA ** (-jnp.arange(0, DIFF_QK_DIM, 2, dtype=F32) / DIFF_QK_DIM)
    ang = pos * inv64[None, :]
    c, s, z = jnp.cos(ang), jnp.sin(ang), jnp.zeros_like(ang)
    cos_a = jnp.concatenate([c, c, c, c], axis=-1)
    sin_a_up = jnp.concatenate([-s, z, -s, z], axis=-1)
    sin_a_dn = jnp.concatenate([z, s, z, s], axis=-1)
    inv128 = ROPE_THETA ** (-jnp.arange(0, DIL_HEAD_DIM, 2, dtype=F32) / DIL_HEAD_DIM)
    ang = pos * inv128[None, :]
    c, s = jnp.cos(ang), jnp.sin(ang)
    cos_b = jnp.concatenate([c, c], axis=-1)
    sin_b = jnp.concatenate([-s, s], axis=-1)
    return (cos_a, sin_a_up, sin_a_dn), (cos_b, sin_b)


def _rope64(x, cos, sin_up, sin_dn):
    return x * cos + pltpu.roll(x, 96, 1) * sin_up + pltpu.roll(x, 32, 1) * sin_dn


def _rope128(x, cos, sin):
    return x * cos + pltpu.roll(x, 64, 1) * sin


def _normalize(x_ref, g_ref):
    x = x_ref[...]
    ms = jnp.mean(x * x, axis=-1, keepdims=True)
    return (x * lax.rsqrt(ms + RMS_EPS) * g_ref[...]).astype(BF16)


def _proj_diff_kernel(x_ref, g_ref, w_ref, cos_ref, sup_ref, sdn_ref, qt_ref, k_ref, vt_ref, gate_ref):
    z = jnp.dot(_normalize(x_ref, g_ref), w_ref[...], preferred_element_type=F32)
    cos, sup, sdn = cos_ref[...], sup_ref[...], sdn_ref[...]
    qscale = DIFF_QK_DIM**-0.5 * LOG2E
    for h in range(DIFF_HEADS):
        lo = h * LANES
        q = _rope64(z[:, lo:lo + LANES], cos, sup, sdn) * qscale
        qt_ref[lo:lo + LANES, :] = q.T.astype(BF16)
        k = _rope64(z[:, DIFF_WIDTH + lo:DIFF_WIDTH + lo + LANES], cos, sup, sdn)
        k_ref[:, lo:lo + LANES] = k.astype(BF16)
        v = z[:, 2 * DIFF_WIDTH + lo:2 * DIFF_WIDTH + lo + LANES]
        vlo = h * VT_ROWS
        vt_ref[vlo:vlo + LANES, :] = v.T.astype(BF16)
        vt_ref[vlo + LANES:vlo + VT_ROWS, :] = jnp.ones((VT_ROWS - LANES, v.shape[0]), BF16)
    gate_ref[...] = z[:, 3 * DIFF_WIDTH:]


def _proj_dil_kernel(x_ref, g_ref, *refs, dilation, with_gate):
    n_w = 4 if with_gate else 3
    w_refs, (cos_ref, sin_ref), refs = refs[:n_w], refs[n_w:n_w + 2], refs[n_w + 2:]
    q_ref, k_ref, v_ref = refs[:3]
    u = _normalize(x_ref, g_ref)
    zq, zk, zv = (jnp.dot(u, w[...], preferred_element_type=F32) for w in w_refs[:3])
    cos, sin = cos_ref[...], sin_ref[...]
    qscale = DIL_HEAD_DIM**-0.5 * LOG2E
    tm = u.shape[0]
    rows = tm // dilation
    if dilation == 1:
        for h in range(DIL_HEADS):
            lanes = slice(h * LANES, (h + 1) * LANES)
            q_ref[0, :, lanes] = (_rope128(zq[:, lanes], cos, sin) * qscale).astype(BF16)
            k_ref[0, :, lanes] = _rope128(zk[:, lanes], cos, sin).astype(BF16)
        v_ref[0] = zv.astype(BF16)
    else:
        stage_ref = refs[-1]
        for h in range(DIL_HEADS):
            lanes = slice(h * LANES, (h + 1) * LANES)
            stage_ref[h] = _rope128(zq[:, lanes], cos, sin) * qscale
            stage_ref[DIL_HEADS + h] = _rope128(zk[:, lanes], cos, sin)
            stage_ref[2 * DIL_HEADS + h] = zv[:, lanes]
        for r in range(dilation):
            for j, out_ref in enumerate((q_ref, k_ref, v_ref)):
                for h in range(DIL_HEADS):
                    blk = stage_ref[j * DIL_HEADS + h, pl.ds(r, rows, stride=dilation), :]
                    out_ref[r, :, h * LANES:(h + 1) * LANES] = blk.astype(BF16)
    if with_gate:
        refs[3][...] = jnp.dot(u, w_refs[3][...], preferred_element_type=F32)


def _proj_tile(seq):
    return min(512, seq)


def _proj_diff(x, g, w, tabs):
    batch, seq, _ = x.shape
    tm = _proj_tile(seq)
    row = lambda b, i: (b, i, 0)
    col = lambda b, i: (b, 0, i)
    tab_spec = pl.BlockSpec((tm, LANES), lambda b, i: (i, 0))
    return pl.pallas_call(
        _proj_diff_kernel,
        grid=(batch, seq // tm),
        in_specs=[
            pl.BlockSpec((None, tm, D_MODEL), row),
            pl.BlockSpec((1, D_MODEL), lambda b, i: (0, 0)),
            pl.BlockSpec((D_MODEL, 4 * DIFF_WIDTH), lambda b, i: (0, 0)),
            tab_spec, tab_spec, tab_spec,
        ],
        out_specs=[
            pl.BlockSpec((None, DIFF_WIDTH, tm), col),
            pl.BlockSpec((None, tm, DIFF_WIDTH), row),
            pl.BlockSpec((None, DIFF_HEADS * VT_ROWS, tm), col),
            pl.BlockSpec((None, tm, DIFF_WIDTH), row),
        ],
        out_shape=[
            jax.ShapeDtypeStruct((batch, DIFF_WIDTH, seq), BF16),
            jax.ShapeDtypeStruct((batch, seq, DIFF_WIDTH), BF16),
            jax.ShapeDtypeStruct((batch, DIFF_HEADS * VT_ROWS, seq), BF16),
            jax.ShapeDtypeStruct((batch, seq, DIFF_WIDTH), F32),
        ],
        compiler_params=_cparams(("parallel", "parallel")),
        name="proj_diff",
    )(x, g, w, *tabs)


def _proj_dil(x, g, w, tabs, group, dilation, with_gate):
    batch, seq, _ = x.shape
    tm = _proj_tile(seq)
    rows = tm // dilation
    sub = seq // dilation
    row = lambda b, i: (b, i, 0)
    res = lambda b, i: (b, 0, i, 0)
    tab_spec = pl.BlockSpec((tm, LANES), lambda b, i: (i, 0))
    res_spec = pl.BlockSpec((None, dilation, rows, DIL_WIDTH), res)
    res_shape = jax.ShapeDtypeStruct((batch, dilation, sub, DIL_WIDTH), BF16)
    out_specs = [res_spec, res_spec, res_spec]
    out_shape = [res_shape, res_shape, res_shape]
    if with_gate:
        out_specs.append(pl.BlockSpec((None, tm, DIL_WIDTH), row))
        out_shape.append(jax.ShapeDtypeStruct((batch, seq, DIL_WIDTH), F32))
    scratch = [] if dilation == 1 else [pltpu.VMEM((3 * DIL_HEADS, tm, LANES), F32)]
    groups = len(DIL_PATTERNS)
    col_blocks = [4 + j * groups + group for j in range(3)] + ([4 + 3 * groups] if with_gate else [])
    w_specs = [pl.BlockSpec((D_MODEL, DIL_WIDTH), functools.partial(lambda b, i, c: (0, c), c=c))
               for c in col_blocks]
    return pl.pallas_call(
        functools.partial(_proj_dil_kernel, dilation=dilation, with_gate=with_gate),
        grid=(batch, seq // tm),
        in_specs=[
            pl.BlockSpec((None, tm, D_MODEL), row),
            pl.BlockSpec((1, D_MODEL), lambda b, i: (0, 0)),
            *w_specs,
            tab_spec, tab_spec,
        ],
        out_specs=out_specs,
        out_shape=out_shape,
        scratch_shapes=scratch,
        compiler_params=_cparams(("parallel", "parallel")),
        name=f"proj_dil{dilation}",
    )(x, g, *([w] * len(w_specs)), *tabs)


def _diff_attn_kernel(lam_ref, subln_ref, qt_ref, k_ref, vt_ref, gate_ref, o_ref,
                      w_scr, s_scr, bm_scr, m_scr, acc_scr, *, tk, unroll, lam_init):
    nk = k_ref.shape[0] // tk
    qt = qt_ref[...]
    row = lax.broadcasted_iota(jnp.int32, qt.shape, 0)
    zero = jnp.zeros_like(qt)
    w_scr[0] = jnp.where(row < DIFF_QK_DIM, qt, zero)
    w_scr[1] = jnp.where(row >= DIFF_QK_DIM, qt, zero)
    m_scr[...] = jnp.full(m_scr.shape, NEG, F32)
    acc_scr[...] = jnp.zeros(acc_scr.shape, F32)

    def scores(j, slot):
        start = pl.multiple_of(j * tk, tk)
        kb = k_ref[pl.ds(start, tk), :]
        for c in range(2):
            s = jnp.dot(kb, w_scr[c], preferred_element_type=F32)
            s_scr[slot, c] = s
            bm_scr[slot, c] = jnp.max(s, axis=0, keepdims=True)

    def accumulate(j, slot):
        start = pl.multiple_of(j * tk, tk)
        vb = vt_ref[:, pl.ds(start, tk)]
        for c in range(2):
            m_old = m_scr[c]
            m_new = jnp.maximum(m_old, bm_scr[slot, c])
            alpha = jnp.exp2(m_old - m_new)
            p = jnp.exp2(s_scr[slot, c] - m_new).astype(BF16)
            acc_scr[c] = alpha * acc_scr[c] + jnp.dot(vb, p, preferred_element_type=F32)
            m_scr[c] = m_new

    scores(0, 0)

    def body(i, carry):
        j = unroll * i
        for u in range(unroll):
            scores(jnp.minimum(j + u + 1, nk - 1), (u + 1) % 2)
            accumulate(j + u, u % 2)
        return carry

    lax.fori_loop(0, nk // unroll, body, 0)

    lam_p = lam_ref[...]
    lam = (jnp.exp(jnp.sum(lam_p[0:1] * lam_p[1:2], axis=-1, keepdims=True))
           - jnp.exp(jnp.sum(lam_p[2:3] * lam_p[3:4], axis=-1, keepdims=True)) + lam_init)
    a0, a1 = acc_scr[0], acc_scr[1]
    o_t = (a0[:DIFF_V_DIM] * (1.0 / a0[DIFF_V_DIM:DIFF_V_DIM + 1])
           - lam * (a1[:DIFF_V_DIM] * (1.0 / a1[DIFF_V_DIM:DIFF_V_DIM + 1])))
    ms = jnp.mean(o_t * o_t, axis=0, keepdims=True)
    o = (o_t * lax.rsqrt(ms + RMS_EPS)).T
    g = gate_ref[...]
    silu = g / (1.0 + jnp.exp(-g))
    o_ref[...] = (o * (subln_ref[...] * (1.0 - lam_init)) * silu).astype(o_ref.dtype)


def _diff_attention(lam_params, subln, qt, k, vt, gate, lam_init):
    batch, seq, _ = k.shape
    tq = min(512, seq)
    tk = min(512, seq // 2)
    unroll = 8 if seq % (8 * tk) == 0 else 2
    assert seq % (unroll * tk) == 0
    return pl.pallas_call(
        functools.partial(_diff_attn_kernel, tk=tk, unroll=unroll, lam_init=lam_init),
        grid=(batch, DIFF_HEADS, seq // tq),
        in_specs=[
            pl.BlockSpec(lam_params.shape, lambda b, h, i: (0, 0)),
            pl.BlockSpec(subln.shape, lambda b, h, i: (0, 0)),
            pl.BlockSpec((None, LANES, tq), lambda b, h, i: (b, h, i)),
            pl.BlockSpec((None, seq, LANES), lambda b, h, i: (b, 0, h)),
            pl.BlockSpec((None, VT_ROWS, seq), lambda b, h, i: (b, h, 0)),
            pl.BlockSpec((None, tq, LANES), lambda b, h, i: (b, i, h)),
        ],
        out_specs=pl.BlockSpec((None, tq, LANES), lambda b, h, i: (b, i, h)),
        out_shape=jax.ShapeDtypeStruct((batch, seq, DIFF_WIDTH), BF16),
        scratch_shapes=[
            pltpu.VMEM((2, LANES, tq), BF16),
            pltpu.VMEM((2, 2, tk, tq), F32),
            pltpu.VMEM((2, 2, 1, tq), F32),
            pltpu.VMEM((2, 1, tq), F32),
            pltpu.VMEM((2, VT_ROWS, tq), F32),
        ],
        compiler_params=_cparams(("parallel", "parallel", "parallel")),
        name="diff_attn",
    )(lam_params, subln, qt, k, vt, gate)


def _band_window(prev_ref, cur_ref, next_ref, unit, n_units, lo):
    r, q = DIL_RADIUS, BAND_UNIT
    lanes = slice(lo, lo + LANES)
    pieces = []
    if unit == 0:
        pieces.append(prev_ref[:, lanes])
    first = max(unit * q - r, 0)
    last = min(unit * q + q + r, n_units * q)
    pieces.append(cur_ref[first:last, lanes])
    if unit == n_units - 1:
        pieces.append(next_ref[:, lanes])
    return pieces[0] if len(pieces) == 1 else jnp.concatenate(pieces, axis=0)


def _band_attn_kernel(q_ref, kp_ref, kc_ref, kn_ref, vp_ref, vc_ref, vn_ref, o_ref, lse_ref, *, sub_len):
    tq = q_ref.shape[0]
    n_units = tq // BAND_UNIT
    tw = BAND_UNIT + 2 * DIL_RADIUS
    i = pl.program_id(1)
    row = lax.broadcasted_iota(jnp.int32, (BAND_UNIT, tw), 0)
    col = lax.broadcasted_iota(jnp.int32, (BAND_UNIT, tw), 1)
    rel = col - row
    band = jnp.where(rel >= 0, 0.0, NEG)
    band = jnp.where(rel <= 2 * DIL_RADIUS, band, NEG).astype(F32)
    for u in range(n_units):
        bias = band
        kpos = i * tq + (u * BAND_UNIT - DIL_RADIUS) + col
        if u == 0:
            bias = jnp.where(kpos >= 0, bias, NEG)
        if u == n_units - 1:
            bias = jnp.where(kpos < sub_len, bias, NEG)
        rows = slice(u * BAND_UNIT, (u + 1) * BAND_UNIT)
        for h in range(DIL_HEADS):
            lo = h * LANES
            kwin = _band_window(kp_ref, kc_ref, kn_ref, u, n_units, lo)
            vwin = _band_window(vp_ref, vc_ref, vn_ref, u, n_units, lo)
            s = lax.dot_general(q_ref[rows, lo:lo + LANES], kwin,
                                (((1,), (1,)), ((), ())), preferred_element_type=F32)
            s = s + bias
            m = jnp.max(s, axis=-1, keepdims=True)
            p = jnp.exp2(s - m)
            den = jnp.sum(p, axis=-1, keepdims=True)
            o = jnp.dot(p.astype(BF16), vwin, preferred_element_type=F32)
            o_ref[rows, lo:lo + LANES] = o * (1.0 / den)
            lse = (m + jnp.log2(den)) * LN2
            lse_ref[rows, h * LSE_LANES:(h + 1) * LSE_LANES] = jnp.broadcast_to(lse, (BAND_UNIT, LSE_LANES))


def _band_attention(q, k, v):
    nseq, sub, _ = q.shape
    tq = min(512, sub)
    assert tq % BAND_UNIT == 0 and sub % tq == 0
    per = tq // DIL_RADIUS
    nhalo = sub // DIL_RADIUS
    cur = pl.BlockSpec((None, tq, DIL_WIDTH), lambda n, i: (n, i, 0))
    prev = pl.BlockSpec((None, DIL_RADIUS, DIL_WIDTH), lambda n, i: (n, jnp.maximum(i * per - 1, 0), 0))
    nxt = pl.BlockSpec((None, DIL_RADIUS, DIL_WIDTH),
                       lambda n, i: (n, jnp.minimum((i + 1) * per, nhalo - 1), 0))
    shape = jax.ShapeDtypeStruct((nseq, sub, DIL_WIDTH), F32)
    lse_spec = pl.BlockSpec((None, tq, LANES), lambda n, i: (n, i, 0))
    lse_shape = jax.ShapeDtypeStruct((nseq, sub, LANES), F32)
    return pl.pallas_call(
        functools.partial(_band_attn_kernel, sub_len=sub),
        grid=(nseq, sub // tq),
        in_specs=[cur, prev, cur, nxt, prev, cur, nxt],
        out_specs=[cur, lse_spec],
        out_shape=[shape, lse_shape],
        compiler_params=_cparams(("parallel", "parallel")),
        name=f"band_attn{sub}",
    )(q, k, k, k, v, v, v)


def _rms(x, g):
    ms = jnp.mean(x * x, axis=-1, keepdims=True)
    return x * lax.rsqrt(ms + RMS_EPS) * g


def _row_parts(tm):
    part = tm // 2 if tm >= 2 * MXU_DIM else tm
    return [slice(r0, r0 + part) for r0 in range(0, tm, part)]


def _tail_kernel(x_ref, ya_ref, o0_ref, l0_ref, o1_ref, l1_ref, o2_ref, l2_ref, bg_ref, p_ref,
                 wout_ref, wgate_ref, wproj_ref, plen_ref, fnorm_ref, y_ref,
                 o1_scr, l1_scr, o2_scr, l2_scr, yb_scr):
    tm = x_ref.shape[0]
    for src, dst in ((o1_ref, o1_scr), (o2_ref, o2_scr)):
        d = src.shape[0]
        for r in range(d):
            for h in range(DIL_HEADS):
                dst[h, pl.ds(r, tm // d, stride=d), :] = src[r, :, h * LANES:(h + 1) * LANES]
    for src, dst in ((l1_ref, l1_scr), (l2_ref, l2_scr)):
        d = src.shape[0]
        for r in range(d):
            dst[pl.ds(r, tm // d, stride=d), :] = src[r]
    for rows in _row_parts(tm):
        n = rows.stop - rows.start
        l0, l1, l2 = l0_ref[rows, :], l1_scr[rows, :], l2_scr[rows, :]
        mx = jnp.maximum(jnp.maximum(l0, l1), l2)
        e0, e1, e2 = jnp.exp(l0 - mx), jnp.exp(l1 - mx), jnp.exp(l2 - mx)
        inv = 1.0 / (e0 + e1 + e2)
        w0, w1, w2 = e0 * inv, e1 * inv, e2 * inv
        for h in range(DIL_HEADS):
            slab = slice(h * LANES, (h + 1) * LANES)
            wide = lambda w: jnp.broadcast_to(w[:, h * LSE_LANES:h * LSE_LANES + 1], (n, LANES))
            ob = wide(w0) * o0_ref[rows, slab] + wide(w1) * o1_scr[h, rows, :] + wide(w2) * o2_scr[h, rows, :]
            bg = bg_ref[rows, slab]
            yb_scr[rows, slab] = (ob * (bg / (1.0 + jnp.exp(-bg)))).astype(BF16)
        hid = (x_ref[rows, :]
               + jnp.dot(ya_ref[rows, :], wout_ref[:DIFF_WIDTH, :], preferred_element_type=F32)
               + jnp.dot(yb_scr[rows, :], wout_ref[DIFF_WIDTH:, :], preferred_element_type=F32))
        nrm = _rms(hid, plen_ref[...]).astype(BF16)
        gate = 1.0 / (1.0 + jnp.exp(-jnp.dot(nrm, wgate_ref[...], preferred_element_type=F32)))
        hid = hid + gate * jnp.dot(p_ref[rows, :].astype(BF16), wproj_ref[...], preferred_element_type=F32)
        y_ref[rows, :] = _rms(hid, fnorm_ref[...])


def _tail(x, ya, o0, l0, o1, l1, o2, l2, bg, p, wout, wgate, wproj, plen, fnorm):
    batch, seq, _ = x.shape
    tm = min(512, seq)
    row = lambda b, i: (b, i, 0)
    full = lambda a: pl.BlockSpec(a.shape, lambda b, i: (0,) * a.ndim)
    wide = pl.BlockSpec((None, tm, D_MODEL), row)
    half = pl.BlockSpec((None, tm, DIL_WIDTH), row)
    lse = pl.BlockSpec((None, tm, LANES), row)

    def res(a):
        d, width = a.shape[1], a.shape[3]
        return pl.BlockSpec((None, d, tm // d, width), lambda b, i: (b, 0, i, 0))

    return pl.pallas_call(
        _tail_kernel,
        grid=(batch, seq // tm),
        in_specs=[wide, half, half, lse, res(o1), res(l1), res(o2), res(l2), half,
                  pl.BlockSpec((None, tm, PLE_DIM), row),
                  full(wout), full(wgate), full(wproj), full(plen), full(fnorm)],
        out_specs=wide,
        out_shape=jax.ShapeDtypeStruct((batch, seq, D_MODEL), F32),
        scratch_shapes=[pltpu.VMEM((DIL_HEADS, tm, LANES), F32), pltpu.VMEM((tm, LANES), F32),
                        pltpu.VMEM((DIL_HEADS, tm, LANES), F32), pltpu.VMEM((tm, LANES), F32),
                        pltpu.VMEM((tm, DIL_WIDTH), BF16)],
        compiler_params=_cparams(("parallel", "parallel")),
        name="tail",
    )(x, ya, o0, l0, o1, l1, o2, l2, bg, p, wout, wgate, wproj, plen, fnorm)


def _layer(x, p, layer_idx, tabs, norm_mix, w_in, lam_params, subln, wout, wgate, wproj, plen, fnorm):
    batch, seq, _ = x.shape
    tabs_a, tabs_b = tabs
    lam_init = 0.8 - 0.6 * math.exp(-0.3 * layer_idx)

    qt, k, vt, ag = _proj_diff(x, norm_mix, w_in, tabs_a)
    ya = _diff_attention(lam_params, subln, qt, k, vt, ag, lam_init)

    outs, lses, bg = [], [], None
    for g, (_, dilation) in enumerate(DIL_PATTERNS):
        res = _proj_dil(x, norm_mix, w_in, tabs_b, g, dilation, with_gate=(g == 0))
        if g == 0:
            bg = res[3]
        sub = seq // dilation
        q, kk, v = (a.reshape(batch * dilation, sub, DIL_WIDTH) for a in res[:3])
        o, lse = _band_attention(q, kk, v)
        outs.append(o.reshape(batch, dilation, sub, DIL_WIDTH))
        lses.append(lse.reshape(batch, dilation, sub, LANES))

    return _tail(x, ya, outs[0].reshape(batch, seq, DIL_WIDTH), lses[0].reshape(batch, seq, LANES),
                 outs[1], lses[1], outs[2], lses[2], bg, p, wout, wgate, wproj, plen, fnorm)


def kernel(x_prompt, x_sample, p_prompt, p_sample, norm_mix, w_in, lam_q1, lam_k1, lam_q2, lam_k2,
           subln, w_out, ple_norm, w_ple_gate, w_ple_proj, final_norm):
    depth = w_in.shape[0]
    assert depth == 1, "the fused tail applies the final norm, so it handles a single layer"
    i = 0
    tabs = _rope_tables(max(x_prompt.shape[1], x_sample.shape[1]))
    lam_params = jnp.stack([lam_q1[i], lam_k1[i], lam_q2[i], lam_k2[i]], axis=0)
    args = (tabs, norm_mix[i][None, :], w_in[i].astype(BF16), lam_params, subln[i][None, :],
            w_out[i].astype(BF16), w_ple_gate[i].astype(BF16), w_ple_proj[i].astype(BF16),
            ple_norm[i][None, :], final_norm[None, :])
    y_prompt = _layer(x_prompt, p_prompt[i], i, *args)
    y_sample = _layer(x_sample, p_sample[i], i, *args)
    return (y_prompt, y_sample)
```

```python
import functools
import math

import jax
import jax.numpy as jnp
from jax import lax
from jax.experimental import pallas as pl
from jax.experimental.pallas import tpu as pltpu

F32 = jnp.float32
BF16 = jnp.bfloat16

D_MODEL = 2048
PLE_DIM = 256
ROPE_THETA = 10000.0
RMS_EPS = 1e-6
NEG = -1e30
LOG2E = math.log2(math.e)
LN2 = math.log(2.0)

DIFF_HEADS = 4
DIFF_QK_DIM = 64
DIFF_V_DIM = 128
DIFF_WIDTH = DIFF_HEADS * DIFF_V_DIM
DIL_PATTERNS = ((128, 1), (512, 4), (2048, 16))
DIL_HEADS = 4
DIL_HEAD_DIM = 128
DIL_WIDTH = DIL_HEADS * DIL_HEAD_DIM
DIL_RADIUS = 64
BAND_UNIT = 128
assert all(w // (2 * d) == DIL_RADIUS for w, d in DIL_PATTERNS)

LANES = 128
MXU_DIM = 256
LSE_LANES = LANES // DIL_HEADS
BF16_SUBLANES = 16
VT_ROWS = DIFF_V_DIM + BF16_SUBLANES
VMEM_LIMIT_BYTES = 56 * 1024 * 1024


def _cparams(semantics):
    return pltpu.CompilerParams(dimension_semantics=semantics, vmem_limit_bytes=VMEM_LIMIT_BYTES)


def _rope_tables(seq):
    pos = jnp.arange(seq, dtype=F32)[:, None]
    inv64 = ROPE_THETA ** (-jnp.arange(0, DIFF_QK_DIM, 2, dtype=F32) / DIFF_QK_DIM)
    ang = pos * inv64[None, :]
    c, s, z = jnp.cos(ang), jnp.sin(ang), jnp.zeros_like(ang)
    cos_a = jnp.concatenate([c, c, c, c], axis=-1)
    sin_a_up = jnp.concatenate([-s, z, -s, z], axis=-1)
    sin_a_dn = jnp.concatenate([z, s, z, s], axis=-1)
    inv128 = ROPE_THETA ** (-jnp.arange(0, DIL_HEAD_DIM, 2, dtype=F32) / DIL_HEAD_DIM)
    ang = pos * inv128[None, :]
    c, s = jnp.cos(ang), jnp.sin(ang)
    cos_b = jnp.concatenate([c, c], axis=-1)
    sin_b = jnp.concatenate([-s, s], axis=-1)
    return (cos_a, sin_a_up, sin_a_dn), (cos_b, sin_b)


def _rope64(x, cos, sin_up, sin_dn):
    return x * cos + pltpu.roll(x, 96, 1) * sin_up + pltpu.roll(x, 32, 1) * sin_dn


def _rope128(x, cos, sin):
    return x * cos + pltpu.roll(x, 64, 1) * sin


def _normalize(x_ref, g_ref):
    x = x_ref[...]
    ms = jnp.mean(x * x, axis=-1, keepdims=True)
    return (x * lax.rsqrt(ms + RMS_EPS) * g_ref[...]).astype(BF16)


def _proj_diff_kernel(x_ref, g_ref, w_ref, cos_ref, sup_ref, sdn_ref, qt_ref, k_ref, vt_ref, gate_ref):
    z = jnp.dot(_normalize(x_ref, g_ref), w_ref[...], preferred_element_type=F32)
    cos, sup, sdn = cos_ref[...], sup_ref[...], sdn_ref[...]
    qscale = DIFF_QK_DIM**-0.5 * LOG2E
    for h in range(DIFF_HEADS):
        lo = h * LANES
        q = _rope64(z[:, lo:lo + LANES], cos, sup, sdn) * qscale
        qt_ref[lo:lo + LANES, :] = q.T.astype(BF16)
        k = _rope64(z[:, DIFF_WIDTH + lo:DIFF_WIDTH + lo + LANES], cos, sup, sdn)
        k_ref[:, lo:lo + LANES] = k.astype(BF16)
        v = z[:, 2 * DIFF_WIDTH + lo:2 * DIFF_WIDTH + lo + LANES]
        vlo = h * VT_ROWS
        vt_ref[vlo:vlo + LANES, :] = v.T.astype(BF16)
        vt_ref[vlo + LANES:vlo + VT_ROWS, :] = jnp.ones((VT_ROWS - LANES, v.shape[0]), BF16)
    gate_ref[...] = z[:, 3 * DIFF_WIDTH:]


def _proj_dil_kernel(x_ref, g_ref, *refs, dilation, with_gate):
    n_w = 4 if with_gate else 3
    w_refs, (cos_ref, sin_ref), refs = refs[:n_w], refs[n_w:n_w + 2], refs[n_w + 2:]
    q_ref, k_ref, v_ref = refs[:3]
    u = _normalize(x_ref, g_ref)
    zq, zk, zv = (jnp.dot(u, w[...], preferred_element_type=F32) for w in w_refs[:3])
    cos, sin = cos_ref[...], sin_ref[...]
    qscale = DIL_HEAD_DIM**-0.5 * LOG2E
    tm = u.shape[0]
    rows = tm // dilation
    if dilation == 1:
        for h in range(DIL_HEADS):
            lanes = slice(h * LANES, (h + 1) * LANES)
            q_ref[0, :, lanes] = (_rope128(zq[:, lanes], cos, sin) * qscale).astype(BF16)
            k_ref[0, :, lanes] = _rope128(zk[:, lanes], cos, sin).astype(BF16)
        v_ref[0] = zv.astype(BF16)
    else:
        stage_ref = refs[-1]
        for h in range(DIL_HEADS):
            lanes = slice(h * LANES, (h + 1) * LANES)
            stage_ref[h] = _rope128(zq[:, lanes], cos, sin) * qscale
            stage_ref[DIL_HEADS + h] = _rope128(zk[:, lanes], cos, sin)
            stage_ref[2 * DIL_HEADS + h] = zv[:, lanes]
        for r in range(dilation):
            for j, out_ref in enumerate((q_ref, k_ref, v_ref)):
                for h in range(DIL_HEADS):
                    blk = stage_ref[j * DIL_HEADS + h, pl.ds(r, rows, stride=dilation), :]
                    out_ref[r, :, h * LANES:(h + 1) * LANES] = blk.astype(BF16)
    if with_gate:
        refs[3][...] = jnp.dot(u, w_refs[3][...], preferred_element_type=F32)


def _proj_tile(seq):
    return min(512, seq)


def _proj_diff(x, g, w, tabs):
    batch, seq, _ = x.shape
    tm = _proj_tile(seq)
    row = lambda b, i: (b, i, 0)
    col = lambda b, i: (b, 0, i)
    tab_spec = pl.BlockSpec((tm, LANES), lambda b, i: (i, 0))
    return pl.pallas_call(
        _proj_diff_kernel,
        grid=(batch, seq // tm),
        in_specs=[
            pl.BlockSpec((None, tm, D_MODEL), row),
            pl.BlockSpec((1, D_MODEL), lambda b, i: (0, 0)),
            pl.BlockSpec((D_MODEL, 4 * DIFF_WIDTH), lambda b, i: (0, 0)),
            tab_spec, tab_spec, tab_spec,
        ],
        out_specs=[
            pl.BlockSpec((None, DIFF_WIDTH, tm), col),
            pl.BlockSpec((None, tm, DIFF_WIDTH), row),
            pl.BlockSpec((None, DIFF_HEADS * VT_ROWS, tm), col),
            pl.BlockSpec((None, tm, DIFF_WIDTH), row),
        ],
        out_shape=[
            jax.ShapeDtypeStruct((batch, DIFF_WIDTH, seq), BF16),
            jax.ShapeDtypeStruct((batch, seq, DIFF_WIDTH), BF16),
            jax.ShapeDtypeStruct((batch, DIFF_HEADS * VT_ROWS, seq), BF16),
            jax.ShapeDtypeStruct((batch, seq, DIFF_WIDTH), F32),
        ],
        compiler_params=_cparams(("parallel", "parallel")),
        name="proj_diff",
    )(x, g, w, *tabs)


def _proj_dil(x, g, w, tabs, group, dilation, with_gate):
    batch, seq, _ = x.shape
    tm = _proj_tile(seq)
    rows = tm // dilation
    sub = seq // dilation
    row = lambda b, i: (b, i, 0)
    res = lambda b, i: (b, 0, i, 0)
    tab_spec = pl.BlockSpec((tm, LANES), lambda b, i: (i, 0))
    res_spec = pl.BlockSpec((None, dilation, rows, DIL_WIDTH), res)
    res_shape = jax.ShapeDtypeStruct((batch, dilation, sub, DIL_WIDTH), BF16)
    out_specs = [res_spec, res_spec, res_spec]
    out_shape = [res_shape, res_shape, res_shape]
    if with_gate:
        out_specs.append(pl.BlockSpec((None, tm, DIL_WIDTH), row))
        out_shape.append(jax.ShapeDtypeStruct((batch, seq, DIL_WIDTH), F32))
    scratch = [] if dilation == 1 else [pltpu.VMEM((3 * DIL_HEADS, tm, LANES), F32)]
    groups = len(DIL_PATTERNS)
    col_blocks = [4 + j * groups + group for j in range(3)] + ([4 + 3 * groups] if with_gate else [])
    w_specs = [pl.BlockSpec((D_MODEL, DIL_WIDTH), functools.partial(lambda b, i, c: (0, c), c=c))
               for c in col_blocks]
    return pl.pallas_call(
        functools.partial(_proj_dil_kernel, dilation=dilation, with_gate=with_gate),
        grid=(batch, seq // tm),
        in_specs=[
            pl.BlockSpec((None, tm, D_MODEL), row),
            pl.BlockSpec((1, D_MODEL), lambda b, i: (0, 0)),
            *w_specs,
            tab_spec, tab_spec,
        ],
        out_specs=out_specs,
        out_shape=out_shape,
        scratch_shapes=scratch,
        compiler_params=_cparams(("parallel", "parallel")),
        name=f"proj_dil{dilation}",
    )(x, g, *([w] * len(w_specs)), *tabs)


def _diff_attn_kernel(lam_ref, subln_ref, qt_ref, k_ref, vt_ref, gate_ref, o_ref,
                      w_scr, s_scr, bm_scr, m_scr, acc_scr, *, tk, unroll, peel, lam_init):
    nk = k_ref.shape[0] // tk
    qt = qt_ref[...]
    row = lax.broadcasted_iota(jnp.int32, qt.shape, 0)
    zero = jnp.zeros_like(qt)
    w_scr[0] = jnp.where(row < DIFF_QK_DIM, qt, zero)
    w_scr[1] = jnp.where(row >= DIFF_QK_DIM, qt, zero)
    m_scr[...] = jnp.full(m_scr.shape, NEG, F32)
    acc_scr[...] = jnp.zeros(acc_scr.shape, F32)

    def scores(j, slot):
        start = pl.multiple_of(j * tk, tk)
        kb = k_ref[pl.ds(start, tk), :]
        for c in range(2):
            s = jnp.dot(kb, w_scr[c], preferred_element_type=F32)
            s_scr[slot, c] = s
            bm_scr[slot, c] = jnp.max(s, axis=0, keepdims=True)

    def accumulate(j, slot):
        start = pl.multiple_of(j * tk, tk)
        vb = vt_ref[:, pl.ds(start, tk)]
        for c in range(2):
            m_old = m_scr[c]
            m_new = jnp.maximum(m_old, bm_scr[slot, c])
            alpha = jnp.exp2(m_old - m_new)
            p = jnp.exp2(s_scr[slot, c] - m_new).astype(BF16)
            acc_scr[c] = alpha * acc_scr[c] + jnp.dot(vb, p, preferred_element_type=F32)
            m_scr[c] = m_new

    scores(0, 0)

    def steps(j, last):
        for u in range(unroll):
            if last and u == unroll - 1:
                pass
            elif peel:
                scores(j + u + 1, (u + 1) % 2)
            else:
                scores(jnp.minimum(j + u + 1, nk - 1), (u + 1) % 2)
            accumulate(j + u, u % 2)

    def body(i, carry):
        steps(unroll * i, last=False)
        return carry

    if peel:
        lax.fori_loop(0, nk // unroll - 1, body, 0)
        steps(nk - unroll, last=True)
    else:
        lax.fori_loop(0, nk // unroll, body, 0)

    lam_p = lam_ref[...]
    lam = (jnp.exp(jnp.sum(lam_p[0:1] * lam_p[1:2], axis=-1, keepdims=True))
           - jnp.exp(jnp.sum(lam_p[2:3] * lam_p[3:4], axis=-1, keepdims=True)) + lam_init)
    a0, a1 = acc_scr[0], acc_scr[1]
    o_t = (a0[:DIFF_V_DIM] * (1.0 / a0[DIFF_V_DIM:DIFF_V_DIM + 1])
           - lam * (a1[:DIFF_V_DIM] * (1.0 / a1[DIFF_V_DIM:DIFF_V_DIM + 1])))
    ms = jnp.mean(o_t * o_t, axis=0, keepdims=True)
    o = (o_t * lax.rsqrt(ms + RMS_EPS)).T
    g = gate_ref[...]
    silu = g / (1.0 + jnp.exp(-g))
    o_ref[...] = (o * (subln_ref[...] * (1.0 - lam_init)) * silu).astype(o_ref.dtype)


def _diff_attention(lam_params, subln, qt, k, vt, gate, lam_init):
    batch, seq, _ = k.shape
    tq = min(512, seq)
    tk = min(512, seq // 2)
    unroll = next(u for u in (8, 4, 2) if seq // tk >= 2 * u)
    peel = seq // (unroll * tk) >= 3
    assert seq % (unroll * tk) == 0
    return pl.pallas_call(
        functools.partial(_diff_attn_kernel, tk=tk, unroll=unroll, peel=peel, lam_init=lam_init),
        grid=(batch, DIFF_HEADS, seq // tq),
        in_specs=[
            pl.BlockSpec(lam_params.shape, lambda b, h, i: (0, 0)),
            pl.BlockSpec(subln.shape, lambda b, h, i: (0, 0)),
            pl.BlockSpec((None, LANES, tq), lambda b, h, i: (b, h, i)),
            pl.BlockSpec((None, seq, LANES), lambda b, h, i: (b, 0, h)),
            pl.BlockSpec((None, VT_ROWS, seq), lambda b, h, i: (b, h, 0)),
            pl.BlockSpec((None, tq, LANES), lambda b, h, i: (b, i, h)),
        ],
        out_specs=pl.BlockSpec((None, tq, LANES), lambda b, h, i: (b, i, h)),
        out_shape=jax.ShapeDtypeStruct((batch, seq, DIFF_WIDTH), BF16),
        scratch_shapes=[
            pltpu.VMEM((2, LANES, tq), BF16),
            pltpu.VMEM((2, 2, tk, tq), F32),
            pltpu.VMEM((2, 2, 1, tq), F32),
            pltpu.VMEM((2, 1, tq), F32),
            pltpu.VMEM((2, VT_ROWS, tq), F32),
        ],
        compiler_params=_cparams(("parallel", "parallel", "parallel")),
        name="diff_attn",
    )(lam_params, subln, qt, k, vt, gate)


def _band_window(prev_ref, cur_ref, next_ref, unit, n_units, lo):
    r, q = DIL_RADIUS, BAND_UNIT
    lanes = slice(lo, lo + LANES)
    pieces = []
    if unit == 0:
        pieces.append(prev_ref[:, lanes])
    first = max(unit * q - r, 0)
    last = min(unit * q + q + r, n_units * q)
    pieces.append(cur_ref[first:last, lanes])
    if unit == n_units - 1:
        pieces.append(next_ref[:, lanes])
    return pieces[0] if len(pieces) == 1 else jnp.concatenate(pieces, axis=0)


def _band_attn_kernel(q_ref, kp_ref, kc_ref, kn_ref, vp_ref, vc_ref, vn_ref, o_ref, lse_ref, *, sub_len):
    tq = q_ref.shape[0]
    n_units = tq // BAND_UNIT
    tw = BAND_UNIT + 2 * DIL_RADIUS
    i = pl.program_id(1)
    row = lax.broadcasted_iota(jnp.int32, (BAND_UNIT, tw), 0)
    col = lax.broadcasted_iota(jnp.int32, (BAND_UNIT, tw), 1)
    rel = col - row
    band = jnp.where(rel >= 0, 0.0, NEG)
    band = jnp.where(rel <= 2 * DIL_RADIUS, band, NEG).astype(F32)
    for u in range(n_units):
        bias = band
        kpos = i * tq + (u * BAND_UNIT - DIL_RADIUS) + col
        if u == 0:
            bias = jnp.where(kpos >= 0, bias, NEG)
        if u == n_units - 1:
            bias = jnp.where(kpos < sub_len, bias, NEG)
        rows = slice(u * BAND_UNIT, (u + 1) * BAND_UNIT)
        for h in range(DIL_HEADS):
            lo = h * LANES
            kwin = _band_window(kp_ref, kc_ref, kn_ref, u, n_units, lo)
            vwin = _band_window(vp_ref, vc_ref, vn_ref, u, n_units, lo)
            s = lax.dot_general(q_ref[rows, lo:lo + LANES], kwin,
                                (((1,), (1,)), ((), ())), preferred_element_type=F32)
            s = s + bias
            m = jnp.max(s, axis=-1, keepdims=True)
            p = jnp.exp2(s - m)
            den = jnp.sum(p, axis=-1, keepdims=True)
            o = jnp.dot(p.astype(BF16), vwin, preferred_element_type=F32)
            o_ref[rows, lo:lo + LANES] = (o * (1.0 / den)).astype(o_ref.dtype)
            lse = (m + jnp.log2(den)) * LN2
            lse_ref[rows, h * LSE_LANES:(h + 1) * LSE_LANES] = jnp.broadcast_to(lse, (BAND_UNIT, LSE_LANES))


def _band_attention(q, k, v):
    nseq, sub, _ = q.shape
    tq = min(512, sub)
    assert tq % BAND_UNIT == 0 and sub % tq == 0
    per = tq // DIL_RADIUS
    nhalo = sub // DIL_RADIUS
    cur = pl.BlockSpec((None, tq, DIL_WIDTH), lambda n, i: (n, i, 0))
    prev = pl.BlockSpec((None, DIL_RADIUS, DIL_WIDTH), lambda n, i: (n, jnp.maximum(i * per - 1, 0), 0))
    nxt = pl.BlockSpec((None, DIL_RADIUS, DIL_WIDTH),
                       lambda n, i: (n, jnp.minimum((i + 1) * per, nhalo - 1), 0))
    shape = jax.ShapeDtypeStruct((nseq, sub, DIL_WIDTH), BF16)
    lse_spec = pl.BlockSpec((None, tq, LANES), lambda n, i: (n, i, 0))
    lse_shape = jax.ShapeDtypeStruct((nseq, sub, LANES), F32)
    return pl.pallas_call(
        functools.partial(_band_attn_kernel, sub_len=sub),
        grid=(nseq, sub // tq),
        in_specs=[cur, prev, cur, nxt, prev, cur, nxt],
        out_specs=[cur, lse_spec],
        out_shape=[shape, lse_shape],
        compiler_params=_cparams(("parallel", "parallel")),
        name=f"band_attn{sub}",
    )(q, k, k, k, v, v, v)


def _rms(x, g):
    ms = jnp.mean(x * x, axis=-1, keepdims=True)
    return x * lax.rsqrt(ms + RMS_EPS) * g


def _row_parts(tm):
    part = tm // 2 if tm >= 2 * MXU_DIM else tm
    return [slice(r0, r0 + part) for r0 in range(0, tm, part)]


def _tail_kernel(x_ref, ya_ref, o0_ref, l0_ref, o1_ref, l1_ref, o2_ref, l2_ref, bg_ref, p_ref,
                 wout_ref, wgate_ref, wproj_ref, plen_ref, fnorm_ref, y_ref,
                 o1_scr, l1_scr, o2_scr, l2_scr, yb_scr):
    tm = x_ref.shape[0]
    for src, dst in ((o1_ref, o1_scr), (o2_ref, o2_scr)):
        d = src.shape[0]
        for r in range(d):
            for h in range(DIL_HEADS):
                dst[h, pl.ds(r, tm // d, stride=d), :] = src[r, :, h * LANES:(h + 1) * LANES].astype(F32)
    for src, dst in ((l1_ref, l1_scr), (l2_ref, l2_scr)):
        d = src.shape[0]
        for r in range(d):
            dst[pl.ds(r, tm // d, stride=d), :] = src[r]
    for rows in _row_parts(tm):
        n = rows.stop - rows.start
        l0, l1, l2 = l0_ref[rows, :], l1_scr[rows, :], l2_scr[rows, :]
        mx = jnp.maximum(jnp.maximum(l0, l1), l2)
        e0, e1, e2 = jnp.exp(l0 - mx), jnp.exp(l1 - mx), jnp.exp(l2 - mx)
        inv = 1.0 / (e0 + e1 + e2)
        w0, w1, w2 = e0 * inv, e1 * inv, e2 * inv
        for h in range(DIL_HEADS):
            slab = slice(h * LANES, (h + 1) * LANES)
            wide = lambda w: jnp.broadcast_to(w[:, h * LSE_LANES:h * LSE_LANES + 1], (n, LANES))
            ob = wide(w0) * o0_ref[rows, slab] + wide(w1) * o1_scr[h, rows, :] + wide(w2) * o2_scr[h, rows, :]
            bg = bg_ref[rows, slab]
            yb_scr[rows, slab] = (ob * (bg / (1.0 + jnp.exp(-bg)))).astype(BF16)
        hid = (x_ref[rows, :]
               + jnp.dot(ya_ref[rows, :], wout_ref[:DIFF_WIDTH, :], preferred_element_type=F32)
               + jnp.dot(yb_scr[rows, :], wout_ref[DIFF_WIDTH:, :], preferred_element_type=F32))
        nrm = _rms(hid, plen_ref[...]).astype(BF16)
        gate = 1.0 / (1.0 + jnp.exp(-jnp.dot(nrm, wgate_ref[...], preferred_element_type=F32)))
        hid = hid + gate * jnp.dot(p_ref[rows, :].astype(BF16), wproj_ref[...], preferred_element_type=F32)
        y_ref[rows, :] = _rms(hid, fnorm_ref[...])


def _tail(x, ya, o0, l0, o1, l1, o2, l2, bg, p, wout, wgate, wproj, plen, fnorm):
    batch, seq, _ = x.shape
    tm = min(512, seq)
    row = lambda b, i: (b, i, 0)
    full = lambda a: pl.BlockSpec(a.shape, lambda b, i: (0,) * a.ndim)
    wide = pl.BlockSpec((None, tm, D_MODEL), row)
    half = pl.BlockSpec((None, tm, DIL_WIDTH), row)
    lse = pl.BlockSpec((None, tm, LANES), row)

    def res(a):
        d, width = a.shape[1], a.shape[3]
        return pl.BlockSpec((None, d, tm // d, width), lambda b, i: (b, 0, i, 0))

    return pl.pallas_call(
        _tail_kernel,
        grid=(batch, seq // tm),
        in_specs=[wide, half, half, lse, res(o1), res(l1), res(o2), res(l2), half,
                  pl.BlockSpec((None, tm, PLE_DIM), row),
                  full(wout), full(wgate), full(wproj), full(plen), full(fnorm)],
        out_specs=wide,
        out_shape=jax.ShapeDtypeStruct((batch, seq, D_MODEL), F32),
        scratch_shapes=[pltpu.VMEM((DIL_HEADS, tm, LANES), F32), pltpu.VMEM((tm, LANES), F32),
                        pltpu.VMEM((DIL_HEADS, tm, LANES), F32), pltpu.VMEM((tm, LANES), F32),
                        pltpu.VMEM((tm, DIL_WIDTH), BF16)],
        compiler_params=_cparams(("parallel", "parallel")),
        name="tail",
    )(x, ya, o0, l0, o1, l1, o2, l2, bg, p, wout, wgate, wproj, plen, fnorm)


def _layer(x, p, layer_idx, tabs, norm_mix, w_in, lam_params, subln, wout, wgate, wproj, plen, fnorm):
    batch, seq, _ = x.shape
    tabs_a, tabs_b = tabs
    lam_init = 0.8 - 0.6 * math.exp(-0.3 * layer_idx)

    qt, k, vt, ag = _proj_diff(x, norm_mix, w_in, tabs_a)
    ya = _diff_attention(lam_params, subln, qt, k, vt, ag, lam_init)

    outs, lses, bg = [], [], None
    for g, (_, dilation) in enumerate(DIL_PATTERNS):
        res = _proj_dil(x, norm_mix, w_in, tabs_b, g, dilation, with_gate=(g == 0))
        if g == 0:
            bg = res[3]
        sub = seq // dilation
        q, kk, v = (a.reshape(batch * dilation, sub, DIL_WIDTH) for a in res[:3])
        o, lse = _band_attention(q, kk, v)
        outs.append(o.reshape(batch, dilation, sub, DIL_WIDTH))
        lses.append(lse.reshape(batch, dilation, sub, LANES))

    return _tail(x, ya, outs[0].reshape(batch, seq, DIL_WIDTH), lses[0].reshape(batch, seq, LANES),
                 outs[1], lses[1], outs[2], lses[2], bg, p, wout, wgate, wproj, plen, fnorm)


def kernel(x_prompt, x_sample, p_prompt, p_sample, norm_mix, w_in, lam_q1, lam_k1, lam_q2, lam_k2,
           subln, w_out, ple_norm, w_ple_gate, w_ple_proj, final_norm):
    depth = w_in.shape[0]
    assert depth == 1, "the fused tail applies the final norm, so it handles a single layer"
    i = 0
    tabs = _rope_tables(max(x_prompt.shape[1], x_sample.shape[1]))
    lam_params = jnp.stack([lam_q1[i], lam_k1[i], lam_q2[i], lam_k2[i]], axis=0)
    args = (tabs, norm_mix[i][None, :], w_in[i].astype(BF16), lam_params, subln[i][None, :],
            w_out[i].astype(BF16), w_ple_gate[i].astype(BF16), w_ple_proj[i].astype(BF16),
            ple_norm[i][None, :], final_norm[None, :])
    y_prompt = _layer(x_prompt, p_prompt[i], i, *args)
    y_sample = _layer(x_sample, p_sample[i], i, *args)
    return (y_prompt, y_sample)
```

```python
import functools
import math

import jax
import jax.numpy as jnp
import numpy as np
from jax import lax
from jax.experimental import pallas as pl
from jax.experimental.pallas import tpu as pltpu

F32 = jnp.float32
BF16 = jnp.bfloat16

D_MODEL = 2048
PLE_DIM = 256
ROPE_THETA = 10000.0
RMS_EPS = 1e-6
NEG = -1e30
LOG2E = math.log2(math.e)
LN2 = math.log(2.0)

DIFF_HEADS = 4
DIFF_QK_DIM = 64
DIFF_V_DIM = 128
DIFF_WIDTH = DIFF_HEADS * DIFF_V_DIM
DIL_PATTERNS = ((128, 1), (512, 4), (2048, 16))
DIL_HEADS = 4
DIL_HEAD_DIM = 128
DIL_WIDTH = DIL_HEADS * DIL_HEAD_DIM
DIL_RADIUS = 64
BAND_UNIT = 128
assert all(w // (2 * d) == DIL_RADIUS for w, d in DIL_PATTERNS)

LANES = 128
MXU_DIM = 256
LSE_LANES = LANES // DIL_HEADS
BF16_SUBLANES = 16
VT_ROWS = DIFF_V_DIM + BF16_SUBLANES
VMEM_LIMIT_BYTES = 56 * 1024 * 1024


def _cparams(semantics):
    return pltpu.CompilerParams(dimension_semantics=semantics, vmem_limit_bytes=VMEM_LIMIT_BYTES)


def _rope_tables(tm, n_blocks):
    def layout(dim, reps):
        inv = np.float32(ROPE_THETA) ** (-np.arange(0, dim, 2, dtype=np.float32) / np.float32(dim))
        freq = np.tile(inv.astype(np.float64), reps)[None, :]
        base = (np.arange(n_blocks, dtype=np.float64) * tm)[:, None] * freq
        row = np.arange(tm, dtype=np.float64)[:, None] * freq
        blk = [jnp.asarray(f(base)[:, None, :], F32) for f in (np.cos, np.sin)]
        return blk + [jnp.asarray(f(row), F32) for f in (np.cos, np.sin)]

    return layout(DIFF_QK_DIM, LANES // (DIFF_QK_DIM // 2)), layout(DIL_HEAD_DIM, LANES // (DIL_HEAD_DIM // 2))


def _angle_add(cb_ref, sb_ref, cr_ref, sr_ref):
    cb, sb, cr, sr = cb_ref[...], sb_ref[...], cr_ref[...], sr_ref[...]
    return cb * cr - sb * sr, sb * cr + cb * sr


def _rope64(x, cos, sin_up, sin_dn):
    return x * cos + pltpu.roll(x, 96, 1) * sin_up + pltpu.roll(x, 32, 1) * sin_dn


def _rope128(x, cos, sin):
    return x * cos + pltpu.roll(x, 64, 1) * sin


def _normalize(x_ref, g_ref):
    x = x_ref[...]
    ms = jnp.mean(x * x, axis=-1, keepdims=True)
    return (x * lax.rsqrt(ms + RMS_EPS) * g_ref[...]).astype(BF16)


def _proj_diff_kernel(x_ref, g_ref, w_ref, cb_ref, sb_ref, cr_ref, sr_ref, qt_ref, k_ref, vt_ref, gate_ref):
    z = jnp.dot(_normalize(x_ref, g_ref), w_ref[...], preferred_element_type=F32)
    cos, sin = _angle_add(cb_ref, sb_ref, cr_ref, sr_ref)
    half = lax.broadcasted_iota(jnp.int32, (1, LANES), 1) % DIFF_QK_DIM < DIFF_QK_DIM // 2
    sup = sin * jnp.where(half, -1.0, 0.0)
    sdn = sin * jnp.where(half, 0.0, 1.0)
    qscale = DIFF_QK_DIM**-0.5 * LOG2E
    for h in range(DIFF_HEADS):
        lo = h * LANES
        q = _rope64(z[:, lo:lo + LANES], cos, sup, sdn) * qscale
        qt_ref[lo:lo + LANES, :] = q.T.astype(BF16)
        k = _rope64(z[:, DIFF_WIDTH + lo:DIFF_WIDTH + lo + LANES], cos, sup, sdn)
        k_ref[:, lo:lo + LANES] = k.astype(BF16)
        v = z[:, 2 * DIFF_WIDTH + lo:2 * DIFF_WIDTH + lo + LANES]
        vlo = h * VT_ROWS
        vt_ref[vlo:vlo + LANES, :] = v.T.astype(BF16)
        vt_ref[vlo + LANES:vlo + VT_ROWS, :] = jnp.ones((VT_ROWS - LANES, v.shape[0]), BF16)
    gate_ref[...] = z[:, 3 * DIFF_WIDTH:]


def _proj_dil_kernel(x_ref, g_ref, *refs, dilation, with_gate):
    n_w = 4 if with_gate else 3
    w_refs, tab_refs, refs = refs[:n_w], refs[n_w:n_w + 4], refs[n_w + 4:]
    q_ref, k_ref, v_ref = refs[:3]
    u = _normalize(x_ref, g_ref)
    zq, zk, zv = (jnp.dot(u, w[...], preferred_element_type=F32) for w in w_refs[:3])
    cos, sin = _angle_add(*tab_refs)
    first = lax.broadcasted_iota(jnp.int32, (1, LANES), 1) < DIL_HEAD_DIM // 2
    sin = sin * jnp.where(first, -1.0, 1.0)
    qscale = DIL_HEAD_DIM**-0.5 * LOG2E
    tm = u.shape[0]
    rows = tm // dilation
    if dilation == 1:
        for h in range(DIL_HEADS):
            lanes = slice(h * LANES, (h + 1) * LANES)
            q_ref[0, :, lanes] = (_rope128(zq[:, lanes], cos, sin) * qscale).astype(BF16)
            k_ref[0, :, lanes] = _rope128(zk[:, lanes], cos, sin).astype(BF16)
        v_ref[0] = zv.astype(BF16)
    else:
        stage_ref = refs[-1]
        for h in range(DIL_HEADS):
            lanes = slice(h * LANES, (h + 1) * LANES)
            stage_ref[h] = _rope128(zq[:, lanes], cos, sin) * qscale
            stage_ref[DIL_HEADS + h] = _rope128(zk[:, lanes], cos, sin)
            stage_ref[2 * DIL_HEADS + h] = zv[:, lanes]
        for r in range(dilation):
            for j, out_ref in enumerate((q_ref, k_ref, v_ref)):
                for h in range(DIL_HEADS):
                    blk = stage_ref[j * DIL_HEADS + h, pl.ds(r, rows, stride=dilation), :]
                    out_ref[r, :, h * LANES:(h + 1) * LANES] = blk.astype(BF16)
    if with_gate:
        refs[3][...] = jnp.dot(u, w_refs[3][...], preferred_element_type=F32)


def _proj_tile(seq):
    return min(512, seq)


def _rope_specs(tm):
    blk = pl.BlockSpec((None, 1, LANES), lambda b, i: (i, 0, 0))
    row = pl.BlockSpec((tm, LANES), lambda b, i: (0, 0))
    return [blk, blk, row, row]


def _proj_diff(x, g, w, tabs):
    batch, seq, _ = x.shape
    tm = _proj_tile(seq)
    row = lambda b, i: (b, i, 0)
    col = lambda b, i: (b, 0, i)
    return pl.pallas_call(
        _proj_diff_kernel,
        grid=(batch, seq // tm),
        in_specs=[
            pl.BlockSpec((None, tm, D_MODEL), row),
            pl.BlockSpec((1, D_MODEL), lambda b, i: (0, 0)),
            pl.BlockSpec((D_MODEL, 4 * DIFF_WIDTH), lambda b, i: (0, 0)),
            *_rope_specs(tm),
        ],
        out_specs=[
            pl.BlockSpec((None, DIFF_WIDTH, tm), col),
            pl.BlockSpec((None, tm, DIFF_WIDTH), row),
            pl.BlockSpec((None, DIFF_HEADS * VT_ROWS, tm), col),
            pl.BlockSpec((None, tm, DIFF_WIDTH), row),
        ],
        out_shape=[
            jax.ShapeDtypeStruct((batch, DIFF_WIDTH, seq), BF16),
            jax.ShapeDtypeStruct((batch, seq, DIFF_WIDTH), BF16),
            jax.ShapeDtypeStruct((batch, DIFF_HEADS * VT_ROWS, seq), BF16),
            jax.ShapeDtypeStruct((batch, seq, DIFF_WIDTH), F32),
        ],
        compiler_params=_cparams(("parallel", "parallel")),
        name="proj_diff",
    )(x, g, w, *tabs)


def _proj_dil(x, g, w, tabs, group, dilation, with_gate):
    batch, seq, _ = x.shape
    tm = _proj_tile(seq)
    rows = tm // dilation
    sub = seq // dilation
    row = lambda b, i: (b, i, 0)
    res = lambda b, i: (b, 0, i, 0)
    res_spec = pl.BlockSpec((None, dilation, rows, DIL_WIDTH), res)
    res_shape = jax.ShapeDtypeStruct((batch, dilation, sub, DIL_WIDTH), BF16)
    out_specs = [res_spec, res_spec, res_spec]
    out_shape = [res_shape, res_shape, res_shape]
    if with_gate:
        out_specs.append(pl.BlockSpec((None, tm, DIL_WIDTH), row))
        out_shape.append(jax.ShapeDtypeStruct((batch, seq, DIL_WIDTH), F32))
    scratch = [] if dilation == 1 else [pltpu.VMEM((3 * DIL_HEADS, tm, LANES), F32)]
    groups = len(DIL_PATTERNS)
    col_blocks = [4 + j * groups + group for j in range(3)] + ([4 + 3 * groups] if with_gate else [])
    w_specs = [pl.BlockSpec((D_MODEL, DIL_WIDTH), functools.partial(lambda b, i, c: (0, c), c=c))
               for c in col_blocks]
    return pl.pallas_call(
        functools.partial(_proj_dil_kernel, dilation=dilation, with_gate=with_gate),
        grid=(batch, seq // tm),
        in_specs=[
            pl.BlockSpec((None, tm, D_MODEL), row),
            pl.BlockSpec((1, D_MODEL), lambda b, i: (0, 0)),
            *w_specs,
            *_rope_specs(tm),
        ],
        out_specs=out_specs,
        out_shape=out_shape,
        scratch_shapes=scratch,
        compiler_params=_cparams(("parallel", "parallel")),
        name=f"proj_dil{dilation}",
    )(x, g, *([w] * len(w_specs)), *tabs)


def _diff_attn_kernel(lam_ref, subln_ref, qt_ref, k_ref, vt_ref, gate_ref, o_ref,
                      w_scr, s_scr, bm_scr, m_scr, acc_scr, *, tk, unroll, peel, lam_init):
    nk = k_ref.shape[0] // tk
    qt = qt_ref[...]
    row = lax.broadcasted_iota(jnp.int32, qt.shape, 0)
    zero = jnp.zeros_like(qt)
    w_scr[0] = jnp.where(row < DIFF_QK_DIM, qt, zero)
    w_scr[1] = jnp.where(row >= DIFF_QK_DIM, qt, zero)
    m_scr[...] = jnp.full(m_scr.shape, NEG, F32)
    acc_scr[...] = jnp.zeros(acc_scr.shape, F32)

    def scores(j, slot):
        start = pl.multiple_of(j * tk, tk)
        kb = k_ref[pl.ds(start, tk), :]
        for c in range(2):
            s = jnp.dot(kb, w_scr[c], preferred_element_type=F32)
            s_scr[slot, c] = s
            bm_scr[slot, c] = jnp.max(s, axis=0, keepdims=True)

    def accumulate(j, slot):
        start = pl.multiple_of(j * tk, tk)
        vb = vt_ref[:, pl.ds(start, tk)]
        for c in range(2):
            m_old = m_scr[c]
            m_new = jnp.maximum(m_old, bm_scr[slot, c])
            alpha = jnp.exp2(m_old - m_new)
            p = jnp.exp2(s_scr[slot, c] - m_new).astype(BF16)
            acc_scr[c] = alpha * acc_scr[c] + jnp.dot(vb, p, preferred_element_type=F32)
            m_scr[c] = m_new

    scores(0, 0)

    def steps(j, last):
        for u in range(unroll):
            if last and u == unroll - 1:
                pass
            elif peel:
                scores(j + u + 1, (u + 1) % 2)
            else:
                scores(jnp.minimum(j + u + 1, nk - 1), (u + 1) % 2)
            accumulate(j + u, u % 2)

    def body(i, carry):
        steps(unroll * i, last=False)
        return carry

    if peel:
        lax.fori_loop(0, nk // unroll - 1, body, 0)
        steps(nk - unroll, last=True)
    else:
        lax.fori_loop(0, nk // unroll, body, 0)

    lam_p = lam_ref[...]
    lam = (jnp.exp(jnp.sum(lam_p[0:1] * lam_p[1:2], axis=-1, keepdims=True))
           - jnp.exp(jnp.sum(lam_p[2:3] * lam_p[3:4], axis=-1, keepdims=True)) + lam_init)
    a0, a1 = acc_scr[0], acc_scr[1]
    o_t = (a0[:DIFF_V_DIM] * (1.0 / a0[DIFF_V_DIM:DIFF_V_DIM + 1])
           - lam * (a1[:DIFF_V_DIM] * (1.0 / a1[DIFF_V_DIM:DIFF_V_DIM + 1])))
    ms = jnp.mean(o_t * o_t, axis=0, keepdims=True)
    o = (o_t * lax.rsqrt(ms + RMS_EPS)).T
    g = gate_ref[...]
    silu = g / (1.0 + jnp.exp(-g))
    o_ref[...] = (o * (subln_ref[...] * (1.0 - lam_init)) * silu).astype(o_ref.dtype)


def _diff_attention(lam_params, subln, qt, k, vt, gate, lam_init):
    batch, seq, _ = k.shape
    tq = min(512, seq)
    tk = min(512, seq // 2)
    unroll = next(u for u in (8, 4, 2) if seq // tk >= 2 * u)
    peel = seq // (unroll * tk) >= 3
    assert seq % (unroll * tk) == 0
    return pl.pallas_call(
        functools.partial(_diff_attn_kernel, tk=tk, unroll=unroll, peel=peel, lam_init=lam_init),
        grid=(batch, DIFF_HEADS, seq // tq),
        in_specs=[
            pl.BlockSpec(lam_params.shape, lambda b, h, i: (0, 0)),
            pl.BlockSpec(subln.shape, lambda b, h, i: (0, 0)),
            pl.BlockSpec((None, LANES, tq), lambda b, h, i: (b, h, i)),
            pl.BlockSpec((None, seq, LANES), lambda b, h, i: (b, 0, h)),
            pl.BlockSpec((None, VT_ROWS, seq), lambda b, h, i: (b, h, 0)),
            pl.BlockSpec((None, tq, LANES), lambda b, h, i: (b, i, h)),
        ],
        out_specs=pl.BlockSpec((None, tq, LANES), lambda b, h, i: (b, i, h)),
        out_shape=jax.ShapeDtypeStruct((batch, seq, DIFF_WIDTH), BF16),
        scratch_shapes=[
            pltpu.VMEM((2, LANES, tq), BF16),
            pltpu.VMEM((2, 2, tk, tq), F32),
            pltpu.VMEM((2, 2, 1, tq), F32),
            pltpu.VMEM((2, 1, tq), F32),
            pltpu.VMEM((2, VT_ROWS, tq), F32),
        ],
        compiler_params=_cparams(("parallel", "parallel", "parallel")),
        name="diff_attn",
    )(lam_params, subln, qt, k, vt, gate)


def _band_window(prev_ref, cur_ref, next_ref, unit, n_units, lo):
    r, q = DIL_RADIUS, BAND_UNIT
    lanes = slice(lo, lo + LANES)
    pieces = []
    if unit == 0:
        pieces.append(prev_ref[:, lanes])
    first = max(unit * q - r, 0)
    last = min(unit * q + q + r, n_units * q)
    pieces.append(cur_ref[first:last, lanes])
    if unit == n_units - 1:
        pieces.append(next_ref[:, lanes])
    return pieces[0] if len(pieces) == 1 else jnp.concatenate(pieces, axis=0)


def _band_attn_kernel(q_ref, kp_ref, kc_ref, kn_ref, vp_ref, vc_ref, vn_ref, o_ref, lse_ref, *, sub_len):
    tq = q_ref.shape[0]
    n_units = tq // BAND_UNIT
    tw = BAND_UNIT + 2 * DIL_RADIUS
    i = pl.program_id(1)
    row = lax.broadcasted_iota(jnp.int32, (BAND_UNIT, tw), 0)
    col = lax.broadcasted_iota(jnp.int32, (BAND_UNIT, tw), 1)
    rel = col - row
    band = jnp.where(rel >= 0, 0.0, NEG)
    band = jnp.where(rel <= 2 * DIL_RADIUS, band, NEG).astype(F32)
    for u in range(n_units):
        bias = band
        kpos = i * tq + (u * BAND_UNIT - DIL_RADIUS) + col
        if u == 0:
            bias = jnp.where(kpos >= 0, bias, NEG)
        if u == n_units - 1:
            bias = jnp.where(kpos < sub_len, bias, NEG)
        rows = slice(u * BAND_UNIT, (u + 1) * BAND_UNIT)
        for h in range(DIL_HEADS):
            lo = h * LANES
            kwin = _band_window(kp_ref, kc_ref, kn_ref, u, n_units, lo)
            vwin = _band_window(vp_ref, vc_ref, vn_ref, u, n_units, lo)
            s = lax.dot_general(q_ref[rows, lo:lo + LANES], kwin,
                                (((1,), (1,)), ((), ())), preferred_element_type=F32)
            s = s + bias
            m = jnp.max(s, axis=-1, keepdims=True)
            p = jnp.exp2(s - m)
            den = jnp.sum(p, axis=-1, keepdims=True)
            o = jnp.dot(p.astype(BF16), vwin, preferred_element_type=F32)
            o_ref[rows, lo:lo + LANES] = (o * (1.0 / den)).astype(o_ref.dtype)
            lse = (m + jnp.log2(den)) * LN2
            lse_ref[rows, h * LSE_LANES:(h + 1) * LSE_LANES] = jnp.broadcast_to(lse, (BAND_UNIT, LSE_LANES))


def _band_attention(q, k, v):
    nseq, sub, _ = q.shape
    tq = min(512, sub)
    assert tq % BAND_UNIT == 0 and sub % tq == 0
    per = tq // DIL_RADIUS
    nhalo = sub // DIL_RADIUS
    cur = pl.BlockSpec((None, tq, DIL_WIDTH), lambda n, i: (n, i, 0))
    prev = pl.BlockSpec((None, DIL_RADIUS, DIL_WIDTH), lambda n, i: (n, jnp.maximum(i * per - 1, 0), 0))
    nxt = pl.BlockSpec((None, DIL_RADIUS, DIL_WIDTH),
                       lambda n, i: (n, jnp.minimum((i + 1) * per, nhalo - 1), 0))
    shape = jax.ShapeDtypeStruct((nseq, sub, DIL_WIDTH), BF16)
    lse_spec = pl.BlockSpec((None, tq, LANES), lambda n, i: (n, i, 0))
    lse_shape = jax.ShapeDtypeStruct((nseq, sub, LANES), F32)
    return pl.pallas_call(
        functools.partial(_band_attn_kernel, sub_len=sub),
        grid=(nseq, sub // tq),
        in_specs=[cur, prev, cur, nxt, prev, cur, nxt],
        out_specs=[cur, lse_spec],
        out_shape=[shape, lse_shape],
        compiler_params=_cparams(("parallel", "parallel")),
        name=f"band_attn{sub}",
    )(q, k, k, k, v, v, v)


def _rms(x, g):
    ms = jnp.mean(x * x, axis=-1, keepdims=True)
    return x * lax.rsqrt(ms + RMS_EPS) * g


def _row_parts(tm):
    part = tm // 2 if tm >= 2 * MXU_DIM else tm
    return [slice(r0, r0 + part) for r0 in range(0, tm, part)]


def _tail_kernel(x_ref, ya_ref, o0_ref, l0_ref, o1_ref, l1_ref, o2_ref, l2_ref, bg_ref, p_ref,
                 wout_ref, wgate_ref, wproj_ref, plen_ref, fnorm_ref, y_ref,
                 o1_scr, l1_scr, o2_scr, l2_scr, yb_scr):
    tm = x_ref.shape[0]
    for src, dst in ((o1_ref, o1_scr), (o2_ref, o2_scr)):
        d = src.shape[0]
        for r in range(d):
            for h in range(DIL_HEADS):
                dst[h, pl.ds(r, tm // d, stride=d), :] = src[r, :, h * LANES:(h + 1) * LANES].astype(F32)
    for src, dst in ((l1_ref, l1_scr), (l2_ref, l2_scr)):
        d = src.shape[0]
        for r in range(d):
            dst[pl.ds(r, tm // d, stride=d), :] = src[r]
    for rows in _row_parts(tm):
        n = rows.stop - rows.start
        l0, l1, l2 = l0_ref[rows, :], l1_scr[rows, :], l2_scr[rows, :]
        mx = jnp.maximum(jnp.maximum(l0, l1), l2)
        e0, e1, e2 = jnp.exp(l0 - mx), jnp.exp(l1 - mx), jnp.exp(l2 - mx)
        inv = 1.0 / (e0 + e1 + e2)
        w0, w1, w2 = e0 * inv, e1 * inv, e2 * inv
        for h in range(DIL_HEADS):
            slab = slice(h * LANES, (h + 1) * LANES)
            wide = lambda w: jnp.broadcast_to(w[:, h * LSE_LANES:h * LSE_LANES + 1], (n, LANES))
            ob = wide(w0) * o0_ref[rows, slab] + wide(w1) * o1_scr[h, rows, :] + wide(w2) * o2_scr[h, rows, :]
            bg = bg_ref[rows, slab]
            yb_scr[rows, slab] = (ob * (bg / (1.0 + jnp.exp(-bg)))).astype(BF16)
        hid = (x_ref[rows, :]
               + jnp.dot(ya_ref[rows, :], wout_ref[:DIFF_WIDTH, :], preferred_element_type=F32)
               + jnp.dot(yb_scr[rows, :], wout_ref[DIFF_WIDTH:, :], preferred_element_type=F32))
        nrm = _rms(hid, plen_ref[...]).astype(BF16)
        gate = 1.0 / (1.0 + jnp.exp(-jnp.dot(nrm, wgate_ref[...], preferred_element_type=F32)))
        hid = hid + gate * jnp.dot(p_ref[rows, :].astype(BF16), wproj_ref[...], preferred_element_type=F32)
        y_ref[rows, :] = _rms(hid, fnorm_ref[...])


def _tail(x, ya, o0, l0, o1, l1, o2, l2, bg, p, wout, wgate, wproj, plen, fnorm):
    batch, seq, _ = x.shape
    tm = min(512, seq)
    row = lambda b, i: (b, i, 0)
    full = lambda a: pl.BlockSpec(a.shape, lambda b, i: (0,) * a.ndim)
    wide = pl.BlockSpec((None, tm, D_MODEL), row)
    half = pl.BlockSpec((None, tm, DIL_WIDTH), row)
    lse = pl.BlockSpec((None, tm, LANES), row)

    def res(a):
        d, width = a.shape[1], a.shape[3]
        return pl.BlockSpec((None, d, tm // d, width), lambda b, i: (b, 0, i, 0))

    return pl.pallas_call(
        _tail_kernel,
        grid=(batch, seq // tm),
        in_specs=[wide, half, half, lse, res(o1), res(l1), res(o2), res(l2), half,
                  pl.BlockSpec((None, tm, PLE_DIM), row),
                  full(wout), full(wgate), full(wproj), full(plen), full(fnorm)],
        out_specs=wide,
        out_shape=jax.ShapeDtypeStruct((batch, seq, D_MODEL), F32),
        scratch_shapes=[pltpu.VMEM((DIL_HEADS, tm, LANES), F32), pltpu.VMEM((tm, LANES), F32),
                        pltpu.VMEM((DIL_HEADS, tm, LANES), F32), pltpu.VMEM((tm, LANES), F32),
                        pltpu.VMEM((tm, DIL_WIDTH), BF16)],
        compiler_params=_cparams(("parallel", "parallel")),
        name="tail",
    )(x, ya, o0, l0, o1, l1, o2, l2, bg, p, wout, wgate, wproj, plen, fnorm)


def _layer(x, p, layer_idx, tabs, norm_mix, w_in, lam_params, subln, wout, wgate, wproj, plen, fnorm):
    batch, seq, _ = x.shape
    tabs_a, tabs_b = tabs
    lam_init = 0.8 - 0.6 * math.exp(-0.3 * layer_idx)

    qt, k, vt, ag = _proj_diff(x, norm_mix, w_in, tabs_a)
    ya = _diff_attention(lam_params, subln, qt, k, vt, ag, lam_init)

    outs, lses, bg = [], [], None
    for g, (_, dilation) in enumerate(DIL_PATTERNS):
        res = _proj_dil(x, norm_mix, w_in, tabs_b, g, dilation, with_gate=(g == 0))
        if g == 0:
            bg = res[3]
        sub = seq // dilation
        q, kk, v = (a.reshape(batch * dilation, sub, DIL_WIDTH) for a in res[:3])
        o, lse = _band_attention(q, kk, v)
        outs.append(o.reshape(batch, dilation, sub, DIL_WIDTH))
        lses.append(lse.reshape(batch, dilation, sub, LANES))

    return _tail(x, ya, outs[0].reshape(batch, seq, DIL_WIDTH), lses[0].reshape(batch, seq, LANES),
                 outs[1], lses[1], outs[2], lses[2], bg, p, wout, wgate, wproj, plen, fnorm)


def kernel(x_prompt, x_sample, p_prompt, p_sample, norm_mix, w_in, lam_q1, lam_k1, lam_q2, lam_k2,
           subln, w_out, ple_norm, w_ple_gate, w_ple_proj, final_norm):
    depth = w_in.shape[0]
    assert depth == 1, "the fused tail applies the final norm, so it handles a single layer"
    i = 0
    seqs = (x_prompt.shape[1], x_sample.shape[1])
    tm = _proj_tile(min(seqs))
    assert all(_proj_tile(n) == tm for n in seqs)
    tabs = _rope_tables(tm, max(seqs) // tm)
    lam_params = jnp.stack([lam_q1[i], lam_k1[i], lam_q2[i], lam_k2[i]], axis=0)
    args = (tabs, norm_mix[i][None, :], w_in[i].astype(BF16), lam_params, subln[i][None, :],
            w_out[i].astype(BF16), w_ple_gate[i].astype(BF16), w_ple_proj[i].astype(BF16),
            ple_norm[i][None, :], final_norm[None, :])
    y_prompt = _layer(x_prompt, p_prompt[i], i, *args)
    y_sample = _layer(x_sample, p_sample[i], i, *args)
    return (y_prompt, y_sample)
```

```python
import functools
import math

import jax
import jax.numpy as jnp
import numpy as np
from jax import lax
from jax.experimental import pallas as pl
from jax.experimental.pallas import tpu as pltpu

F32 = jnp.float32
BF16 = jnp.bfloat16

D_MODEL = 2048
PLE_DIM = 256
ROPE_THETA = 10000.0
RMS_EPS = 1e-6
NEG = -1e30
LOG2E = math.log2(math.e)
LN2 = math.log(2.0)

DIFF_HEADS = 4
DIFF_QK_DIM = 64
DIFF_V_DIM = 128
DIFF_WIDTH = DIFF_HEADS * DIFF_V_DIM
DIL_PATTERNS = ((128, 1), (512, 4), (2048, 16))
DIL_HEADS = 4
DIL_HEAD_DIM = 128
DIL_WIDTH = DIL_HEADS * DIL_HEAD_DIM
DIL_RADIUS = 64
BAND_UNIT = 128
BAND_ROWS = 2048
assert all(w // (2 * d) == DIL_RADIUS for w, d in DIL_PATTERNS)

LANES = 128
MXU_DIM = 256
LSE_LANES = LANES // DIL_HEADS
BF16_SUBLANES = 16
VT_ROWS = DIFF_V_DIM + BF16_SUBLANES
VMEM_LIMIT_BYTES = 56 * 1024 * 1024


def _cparams(semantics):
    return pltpu.CompilerParams(dimension_semantics=semantics, vmem_limit_bytes=VMEM_LIMIT_BYTES)


def _rope_tables(tm, n_blocks):
    def layout(dim, reps):
        inv = np.float32(ROPE_THETA) ** (-np.arange(0, dim, 2, dtype=np.float32) / np.float32(dim))
        freq = np.tile(inv.astype(np.float64), reps)[None, :]
        base = (np.arange(n_blocks, dtype=np.float64) * tm)[:, None] * freq
        row = np.arange(tm, dtype=np.float64)[:, None] * freq
        blk = [jnp.asarray(f(base)[:, None, :], F32) for f in (np.cos, np.sin)]
        return blk + [jnp.asarray(f(row), F32) for f in (np.cos, np.sin)]

    return layout(DIFF_QK_DIM, LANES // (DIFF_QK_DIM // 2)), layout(DIL_HEAD_DIM, LANES // (DIL_HEAD_DIM // 2))


def _angle_add(cb_ref, sb_ref, cr_ref, sr_ref):
    cb, sb, cr, sr = cb_ref[...], sb_ref[...], cr_ref[...], sr_ref[...]
    return cb * cr - sb * sr, sb * cr + cb * sr


def _rope64(x, cos, sin_up, sin_dn):
    return x * cos + pltpu.roll(x, 96, 1) * sin_up + pltpu.roll(x, 32, 1) * sin_dn


def _rope128(x, cos, sin):
    return x * cos + pltpu.roll(x, 64, 1) * sin


def _normalize(x_ref, g_ref):
    x = x_ref[...]
    ms = jnp.mean(x * x, axis=-1, keepdims=True)
    return (x * lax.rsqrt(ms + RMS_EPS) * g_ref[...]).astype(BF16)


def _proj_diff_kernel(x_ref, g_ref, w_ref, cb_ref, sb_ref, cr_ref, sr_ref, qt_ref, k_ref, vt_ref, gate_ref):
    z = jnp.dot(_normalize(x_ref, g_ref), w_ref[...], preferred_element_type=F32)
    cos, sin = _angle_add(cb_ref, sb_ref, cr_ref, sr_ref)
    half = lax.broadcasted_iota(jnp.int32, (1, LANES), 1) % DIFF_QK_DIM < DIFF_QK_DIM // 2
    sup = sin * jnp.where(half, -1.0, 0.0)
    sdn = sin * jnp.where(half, 0.0, 1.0)
    qscale = DIFF_QK_DIM**-0.5 * LOG2E
    for h in range(DIFF_HEADS):
        lo = h * LANES
        q = _rope64(z[:, lo:lo + LANES], cos, sup, sdn) * qscale
        qt_ref[lo:lo + LANES, :] = q.T.astype(BF16)
        k = _rope64(z[:, DIFF_WIDTH + lo:DIFF_WIDTH + lo + LANES], cos, sup, sdn)
        k_ref[:, lo:lo + LANES] = k.astype(BF16)
        v = z[:, 2 * DIFF_WIDTH + lo:2 * DIFF_WIDTH + lo + LANES]
        vlo = h * VT_ROWS
        vt_ref[vlo:vlo + LANES, :] = v.T.astype(BF16)
        vt_ref[vlo + LANES:vlo + VT_ROWS, :] = jnp.ones((VT_ROWS - LANES, v.shape[0]), BF16)
    gate_ref[...] = z[:, 3 * DIFF_WIDTH:]


def _proj_dil_kernel(x_ref, g_ref, *refs, dilation, with_gate):
    n_w = 4 if with_gate else 3
    w_refs, tab_refs, refs = refs[:n_w], refs[n_w:n_w + 4], refs[n_w + 4:]
    q_ref, k_ref, v_ref = refs[:3]
    u = _normalize(x_ref, g_ref)
    zq, zk, zv = (jnp.dot(u, w[...], preferred_element_type=F32) for w in w_refs[:3])
    cos, sin = _angle_add(*tab_refs)
    first = lax.broadcasted_iota(jnp.int32, (1, LANES), 1) < DIL_HEAD_DIM // 2
    sin = sin * jnp.where(first, -1.0, 1.0)
    qscale = DIL_HEAD_DIM**-0.5 * LOG2E
    tm = u.shape[0]
    rows = tm // dilation
    if dilation == 1:
        for h in range(DIL_HEADS):
            lanes = slice(h * LANES, (h + 1) * LANES)
            q_ref[0, :, lanes] = (_rope128(zq[:, lanes], cos, sin) * qscale).astype(BF16)
            k_ref[0, :, lanes] = _rope128(zk[:, lanes], cos, sin).astype(BF16)
        v_ref[0] = zv.astype(BF16)
    else:
        stage_ref = refs[-1]
        for h in range(DIL_HEADS):
            lanes = slice(h * LANES, (h + 1) * LANES)
            stage_ref[h] = _rope128(zq[:, lanes], cos, sin) * qscale
            stage_ref[DIL_HEADS + h] = _rope128(zk[:, lanes], cos, sin)
            stage_ref[2 * DIL_HEADS + h] = zv[:, lanes]
        for r in range(dilation):
            for j, out_ref in enumerate((q_ref, k_ref, v_ref)):
                for h in range(DIL_HEADS):
                    blk = stage_ref[j * DIL_HEADS + h, pl.ds(r, rows, stride=dilation), :]
                    out_ref[r, :, h * LANES:(h + 1) * LANES] = blk.astype(BF16)
    if with_gate:
        refs[3][...] = jnp.dot(u, w_refs[3][...], preferred_element_type=F32)


def _proj_tile(seq):
    return min(512, seq)


def _rope_specs(tm):
    blk = pl.BlockSpec((None, 1, LANES), lambda b, i: (i, 0, 0))
    row = pl.BlockSpec((tm, LANES), lambda b, i: (0, 0))
    return [blk, blk, row, row]


def _proj_diff(x, g, w, tabs):
    batch, seq, _ = x.shape
    tm = _proj_tile(seq)
    row = lambda b, i: (b, i, 0)
    col = lambda b, i: (b, 0, i)
    return pl.pallas_call(
        _proj_diff_kernel,
        grid=(batch, seq // tm),
        in_specs=[
            pl.BlockSpec((None, tm, D_MODEL), row),
            pl.BlockSpec((1, D_MODEL), lambda b, i: (0, 0)),
            pl.BlockSpec((D_MODEL, 4 * DIFF_WIDTH), lambda b, i: (0, 0)),
            *_rope_specs(tm),
        ],
        out_specs=[
            pl.BlockSpec((None, DIFF_WIDTH, tm), col),
            pl.BlockSpec((None, tm, DIFF_WIDTH), row),
            pl.BlockSpec((None, DIFF_HEADS * VT_ROWS, tm), col),
            pl.BlockSpec((None, tm, DIFF_WIDTH), row),
        ],
        out_shape=[
            jax.ShapeDtypeStruct((batch, DIFF_WIDTH, seq), BF16),
            jax.ShapeDtypeStruct((batch, seq, DIFF_WIDTH), BF16),
            jax.ShapeDtypeStruct((batch, DIFF_HEADS * VT_ROWS, seq), BF16),
            jax.ShapeDtypeStruct((batch, seq, DIFF_WIDTH), F32),
        ],
        compiler_params=_cparams(("parallel", "parallel")),
        name="proj_diff",
    )(x, g, w, *tabs)


def _proj_dil(x, g, w, tabs, group, dilation, with_gate):
    batch, seq, _ = x.shape
    tm = _proj_tile(seq)
    rows = tm // dilation
    sub = seq // dilation
    row = lambda b, i: (b, i, 0)
    res = lambda b, i: (b, 0, i, 0)
    res_spec = pl.BlockSpec((None, dilation, rows, DIL_WIDTH), res)
    res_shape = jax.ShapeDtypeStruct((batch, dilation, sub, DIL_WIDTH), BF16)
    out_specs = [res_spec, res_spec, res_spec]
    out_shape = [res_shape, res_shape, res_shape]
    if with_gate:
        out_specs.append(pl.BlockSpec((None, tm, DIL_WIDTH), row))
        out_shape.append(jax.ShapeDtypeStruct((batch, seq, DIL_WIDTH), F32))
    scratch = [] if dilation == 1 else [pltpu.VMEM((3 * DIL_HEADS, tm, LANES), F32)]
    groups = len(DIL_PATTERNS)
    col_blocks = [4 + j * groups + group for j in range(3)] + ([4 + 3 * groups] if with_gate else [])
    w_specs = [pl.BlockSpec((D_MODEL, DIL_WIDTH), functools.partial(lambda b, i, c: (0, c), c=c))
               for c in col_blocks]
    return pl.pallas_call(
        functools.partial(_proj_dil_kernel, dilation=dilation, with_gate=with_gate),
        grid=(batch, seq // tm),
        in_specs=[
            pl.BlockSpec((None, tm, D_MODEL), row),
            pl.BlockSpec((1, D_MODEL), lambda b, i: (0, 0)),
            *w_specs,
            *_rope_specs(tm),
        ],
        out_specs=out_specs,
        out_shape=out_shape,
        scratch_shapes=scratch,
        compiler_params=_cparams(("parallel", "parallel")),
        name=f"proj_dil{dilation}",
    )(x, g, *([w] * len(w_specs)), *tabs)


def _diff_attn_kernel(lam_ref, subln_ref, qt_ref, k_ref, vt_ref, gate_ref, o_ref,
                      w_scr, s_scr, bm_scr, m_scr, acc_scr, *, tk, unroll, peel, lam_init):
    nk = k_ref.shape[0] // tk
    qt = qt_ref[...]
    row = lax.broadcasted_iota(jnp.int32, qt.shape, 0)
    zero = jnp.zeros_like(qt)
    w_scr[0] = jnp.where(row < DIFF_QK_DIM, qt, zero)
    w_scr[1] = jnp.where(row >= DIFF_QK_DIM, qt, zero)
    m_scr[...] = jnp.full(m_scr.shape, NEG, F32)
    acc_scr[...] = jnp.zeros(acc_scr.shape, F32)

    def scores(j, slot):
        start = pl.multiple_of(j * tk, tk)
        kb = k_ref[pl.ds(start, tk), :]
        for c in range(2):
            s = jnp.dot(kb, w_scr[c], preferred_element_type=F32)
            s_scr[slot, c] = s
            bm_scr[slot, c] = jnp.max(s, axis=0, keepdims=True)

    def accumulate(j, slot):
        start = pl.multiple_of(j * tk, tk)
        vb = vt_ref[:, pl.ds(start, tk)]
        for c in range(2):
            m_old = m_scr[c]
            m_new = jnp.maximum(m_old, bm_scr[slot, c])
            alpha = jnp.exp2(m_old - m_new)
            p = jnp.exp2(s_scr[slot, c] - m_new).astype(BF16)
            acc_scr[c] = alpha * acc_scr[c] + jnp.dot(vb, p, preferred_element_type=F32)
            m_scr[c] = m_new

    scores(0, 0)

    def steps(j, last):
        for u in range(unroll):
            if last and u == unroll - 1:
                pass
            elif peel:
                scores(j + u + 1, (u + 1) % 2)
            else:
                scores(jnp.minimum(j + u + 1, nk - 1), (u + 1) % 2)
            accumulate(j + u, u % 2)

    def body(i, carry):
        steps(unroll * i, last=False)
        return carry

    if peel:
        lax.fori_loop(0, nk // unroll - 1, body, 0)
        steps(nk - unroll, last=True)
    else:
        lax.fori_loop(0, nk // unroll, body, 0)

    lam_p = lam_ref[...]
    lam = (jnp.exp(jnp.sum(lam_p[0:1] * lam_p[1:2], axis=-1, keepdims=True))
           - jnp.exp(jnp.sum(lam_p[2:3] * lam_p[3:4], axis=-1, keepdims=True)) + lam_init)
    a0, a1 = acc_scr[0], acc_scr[1]
    o_t = (a0[:DIFF_V_DIM] * (1.0 / a0[DIFF_V_DIM:DIFF_V_DIM + 1])
           - lam * (a1[:DIFF_V_DIM] * (1.0 / a1[DIFF_V_DIM:DIFF_V_DIM + 1])))
    ms = jnp.mean(o_t * o_t, axis=0, keepdims=True)
    o = (o_t * lax.rsqrt(ms + RMS_EPS)).T
    g = gate_ref[...]
    silu = g / (1.0 + jnp.exp(-g))
    o_ref[...] = (o * (subln_ref[...] * (1.0 - lam_init)) * silu).astype(o_ref.dtype)


def _diff_attention(lam_params, subln, qt, k, vt, gate, lam_init):
    batch, seq, _ = k.shape
    tq = min(512, seq)
    tk = min(512, seq // 2)
    unroll = next(u for u in (8, 4, 2) if seq // tk >= 2 * u)
    peel = seq // (unroll * tk) >= 3
    assert seq % (unroll * tk) == 0
    return pl.pallas_call(
        functools.partial(_diff_attn_kernel, tk=tk, unroll=unroll, peel=peel, lam_init=lam_init),
        grid=(batch, DIFF_HEADS, seq // tq),
        in_specs=[
            pl.BlockSpec(lam_params.shape, lambda b, h, i: (0, 0)),
            pl.BlockSpec(subln.shape, lambda b, h, i: (0, 0)),
            pl.BlockSpec((None, LANES, tq), lambda b, h, i: (b, h, i)),
            pl.BlockSpec((None, seq, LANES), lambda b, h, i: (b, 0, h)),
            pl.BlockSpec((None, VT_ROWS, seq), lambda b, h, i: (b, h, 0)),
            pl.BlockSpec((None, tq, LANES), lambda b, h, i: (b, i, h)),
        ],
        out_specs=pl.BlockSpec((None, tq, LANES), lambda b, h, i: (b, i, h)),
        out_shape=jax.ShapeDtypeStruct((batch, seq, DIFF_WIDTH), BF16),
        scratch_shapes=[
            pltpu.VMEM((2, LANES, tq), BF16),
            pltpu.VMEM((2, 2, tk, tq), F32),
            pltpu.VMEM((2, 2, 1, tq), F32),
            pltpu.VMEM((2, 1, tq), F32),
            pltpu.VMEM((2, VT_ROWS, tq), F32),
        ],
        compiler_params=_cparams(("parallel", "parallel", "parallel")),
        name="diff_attn",
    )(lam_params, subln, qt, k, vt, gate)


def _band_window(prev_ref, cur_ref, next_ref, unit, n_units, lo):
    r, q = DIL_RADIUS, BAND_UNIT
    lanes = slice(lo, lo + LANES)
    pieces = []
    if unit == 0:
        pieces.append(prev_ref[:, lanes])
    first = max(unit * q - r, 0)
    last = min(unit * q + q + r, n_units * q)
    pieces.append(cur_ref[first:last, lanes])
    if unit == n_units - 1:
        pieces.append(next_ref[:, lanes])
    return pieces[0] if len(pieces) == 1 else jnp.concatenate(pieces, axis=0)


def _band_attn_kernel(q_ref, kp_ref, kc_ref, kn_ref, vp_ref, vc_ref, vn_ref, o_ref, lse_ref, *, sub_len):
    t_rows = q_ref.shape[0]
    n_units = t_rows // BAND_UNIT
    period = min(sub_len, t_rows)
    tw = BAND_UNIT + 2 * DIL_RADIUS
    i = pl.program_id(0)
    row = lax.broadcasted_iota(jnp.int32, (BAND_UNIT, tw), 0)
    col = lax.broadcasted_iota(jnp.int32, (BAND_UNIT, tw), 1)
    rel = col - row
    band = jnp.where(rel >= 0, 0.0, NEG)
    band = jnp.where(rel <= 2 * DIL_RADIUS, band, NEG).astype(F32)
    for u in range(n_units):
        bias = band
        offset = (u * BAND_UNIT) % period
        may_start, may_end = offset == 0, offset == period - BAND_UNIT
        if may_start or may_end:
            kpos = lax.rem(i * t_rows + u * BAND_UNIT, sub_len) - DIL_RADIUS + col
            if may_start:
                bias = jnp.where(kpos >= 0, bias, NEG)
            if may_end:
                bias = jnp.where(kpos < sub_len, bias, NEG)
        rows = slice(u * BAND_UNIT, (u + 1) * BAND_UNIT)
        for h in range(DIL_HEADS):
            lo = h * LANES
            kwin = _band_window(kp_ref, kc_ref, kn_ref, u, n_units, lo)
            vwin = _band_window(vp_ref, vc_ref, vn_ref, u, n_units, lo)
            s = lax.dot_general(q_ref[rows, lo:lo + LANES], kwin,
                                (((1,), (1,)), ((), ())), preferred_element_type=F32)
            s = s + bias
            m = jnp.max(s, axis=-1, keepdims=True)
            p = jnp.exp2(s - m)
            den = jnp.sum(p, axis=-1, keepdims=True)
            o = jnp.dot(p.astype(BF16), vwin, preferred_element_type=F32)
            o_ref[rows, lo:lo + LANES] = (o * (1.0 / den)).astype(o_ref.dtype)
            lse = (m + jnp.log2(den)) * LN2
            lse_ref[rows, h * LSE_LANES:(h + 1) * LSE_LANES] = jnp.broadcast_to(lse, (BAND_UNIT, LSE_LANES))


def _band_attention(q, k, v, sub_len):
    n_rows = q.shape[0]
    t_rows = min(BAND_ROWS, n_rows)
    assert n_rows % t_rows == 0 and sub_len % BAND_UNIT == 0
    assert t_rows % sub_len == 0 or sub_len % t_rows == 0
    per = t_rows // DIL_RADIUS
    nhalo = n_rows // DIL_RADIUS
    cur = pl.BlockSpec((t_rows, DIL_WIDTH), lambda i: (i, 0))
    prev = pl.BlockSpec((DIL_RADIUS, DIL_WIDTH), lambda i: (jnp.maximum(i * per - 1, 0), 0))
    nxt = pl.BlockSpec((DIL_RADIUS, DIL_WIDTH), lambda i: (jnp.minimum((i + 1) * per, nhalo - 1), 0))
    return pl.pallas_call(
        functools.partial(_band_attn_kernel, sub_len=sub_len),
        grid=(n_rows // t_rows,),
        in_specs=[cur, prev, cur, nxt, prev, cur, nxt],
        out_specs=[cur, pl.BlockSpec((t_rows, LANES), lambda i: (i, 0))],
        out_shape=[jax.ShapeDtypeStruct((n_rows, DIL_WIDTH), BF16), jax.ShapeDtypeStruct((n_rows, LANES), F32)],
        compiler_params=_cparams(("parallel",)),
        name=f"band_attn{sub_len}",
    )(q, k, k, k, v, v, v)


def _rms(x, g):
    ms = jnp.mean(x * x, axis=-1, keepdims=True)
    return x * lax.rsqrt(ms + RMS_EPS) * g


def _row_parts(tm):
    part = tm // 2 if tm >= 2 * MXU_DIM else tm
    return [slice(r0, r0 + part) for r0 in range(0, tm, part)]


def _tail_kernel(x_ref, ya_ref, o0_ref, l0_ref, o1_ref, l1_ref, o2_ref, l2_ref, bg_ref, p_ref,
                 wout_ref, wgate_ref, wproj_ref, plen_ref, fnorm_ref, y_ref,
                 o1_scr, l1_scr, o2_scr, l2_scr, yb_scr):
    tm = x_ref.shape[0]
    for src, dst in ((o1_ref, o1_scr), (o2_ref, o2_scr)):
        d = src.shape[0]
        for r in range(d):
            for h in range(DIL_HEADS):
                dst[h, pl.ds(r, tm // d, stride=d), :] = src[r, :, h * LANES:(h + 1) * LANES].astype(F32)
    for src, dst in ((l1_ref, l1_scr), (l2_ref, l2_scr)):
        d = src.shape[0]
        for r in range(d):
            dst[pl.ds(r, tm // d, stride=d), :] = src[r]
    for rows in _row_parts(tm):
        n = rows.stop - rows.start
        l0, l1, l2 = l0_ref[rows, :], l1_scr[rows, :], l2_scr[rows, :]
        mx = jnp.maximum(jnp.maximum(l0, l1), l2)
        e0, e1, e2 = jnp.exp(l0 - mx), jnp.exp(l1 - mx), jnp.exp(l2 - mx)
        inv = 1.0 / (e0 + e1 + e2)
        w0, w1, w2 = e0 * inv, e1 * inv, e2 * inv
        for h in range(DIL_HEADS):
            slab = slice(h * LANES, (h + 1) * LANES)
            wide = lambda w: jnp.broadcast_to(w[:, h * LSE_LANES:h * LSE_LANES + 1], (n, LANES))
            ob = wide(w0) * o0_ref[rows, slab] + wide(w1) * o1_scr[h, rows, :] + wide(w2) * o2_scr[h, rows, :]
            bg = bg_ref[rows, slab]
            yb_scr[rows, slab] = (ob * (bg / (1.0 + jnp.exp(-bg)))).astype(BF16)
        hid = (x_ref[rows, :]
               + jnp.dot(ya_ref[rows, :], wout_ref[:DIFF_WIDTH, :], preferred_element_type=F32)
               + jnp.dot(yb_scr[rows, :], wout_ref[DIFF_WIDTH:, :], preferred_element_type=F32))
        nrm = _rms(hid, plen_ref[...]).astype(BF16)
        gate = 1.0 / (1.0 + jnp.exp(-jnp.dot(nrm, wgate_ref[...], preferred_element_type=F32)))
        hid = hid + gate * jnp.dot(p_ref[rows, :].astype(BF16), wproj_ref[...], preferred_element_type=F32)
        y_ref[rows, :] = _rms(hid, fnorm_ref[...])


def _tail(x, ya, o0, l0, o1, l1, o2, l2, bg, p, wout, wgate, wproj, plen, fnorm):
    batch, seq, _ = x.shape
    tm = min(512, seq)
    row = lambda b, i: (b, i, 0)
    full = lambda a: pl.BlockSpec(a.shape, lambda b, i: (0,) * a.ndim)
    wide = pl.BlockSpec((None, tm, D_MODEL), row)
    half = pl.BlockSpec((None, tm, DIL_WIDTH), row)
    lse = pl.BlockSpec((None, tm, LANES), row)

    def res(a):
        d, width = a.shape[1], a.shape[3]
        return pl.BlockSpec((None, d, tm // d, width), lambda b, i: (b, 0, i, 0))

    return pl.pallas_call(
        _tail_kernel,
        grid=(batch, seq // tm),
        in_specs=[wide, half, half, lse, res(o1), res(l1), res(o2), res(l2), half,
                  pl.BlockSpec((None, tm, PLE_DIM), row),
                  full(wout), full(wgate), full(wproj), full(plen), full(fnorm)],
        out_specs=wide,
        out_shape=jax.ShapeDtypeStruct((batch, seq, D_MODEL), F32),
        scratch_shapes=[pltpu.VMEM((DIL_HEADS, tm, LANES), F32), pltpu.VMEM((tm, LANES), F32),
                        pltpu.VMEM((DIL_HEADS, tm, LANES), F32), pltpu.VMEM((tm, LANES), F32),
                        pltpu.VMEM((tm, DIL_WIDTH), BF16)],
        compiler_params=_cparams(("parallel", "parallel")),
        name="tail",
    )(x, ya, o0, l0, o1, l1, o2, l2, bg, p, wout, wgate, wproj, plen, fnorm)


def _layer(x, p, layer_idx, tabs, norm_mix, w_in, lam_params, subln, wout, wgate, wproj, plen, fnorm):
    batch, seq, _ = x.shape
    tabs_a, tabs_b = tabs
    lam_init = 0.8 - 0.6 * math.exp(-0.3 * layer_idx)

    qt, k, vt, ag = _proj_diff(x, norm_mix, w_in, tabs_a)
    ya = _diff_attention(lam_params, subln, qt, k, vt, ag, lam_init)

    outs, lses, bg = [], [], None
    for g, (_, dilation) in enumerate(DIL_PATTERNS):
        res = _proj_dil(x, norm_mix, w_in, tabs_b, g, dilation, with_gate=(g == 0))
        if g == 0:
            bg = res[3]
        sub = seq // dilation
        q, kk, v = (a.reshape(batch * seq, DIL_WIDTH) for a in res[:3])
        o, lse = _band_attention(q, kk, v, sub)
        outs.append(o.reshape(batch, dilation, sub, DIL_WIDTH))
        lses.append(lse.reshape(batch, dilation, sub, LANES))

    return _tail(x, ya, outs[0].reshape(batch, seq, DIL_WIDTH), lses[0].reshape(batch, seq, LANES),
                 outs[1], lses[1], outs[2], lses[2], bg, p, wout, wgate, wproj, plen, fnorm)


def kernel(x_prompt, x_sample, p_prompt, p_sample, norm_mix, w_in, lam_q1, lam_k1, lam_q2, lam_k2,
           subln, w_out, ple_norm, w_ple_gate, w_ple_proj, final_norm):
    depth = w_in.shape[0]
    assert depth == 1, "the fused tail applies the final norm, so it handles a single layer"
    i = 0
    seqs = (x_prompt.shape[1], x_sample.shape[1])
    tm = _proj_tile(min(seqs))
    assert all(_proj_tile(n) == tm for n in seqs)
    tabs = _rope_tables(tm, max(seqs) // tm)
    lam_params = jnp.stack([lam_q1[i], lam_k1[i], lam_q2[i], lam_k2[i]], axis=0)
    args = (tabs, norm_mix[i][None, :], w_in[i].astype(BF16), lam_params, subln[i][None, :],
            w_out[i].astype(BF16), w_ple_gate[i].astype(BF16), w_ple_proj[i].astype(BF16),
            ple_norm[i][None, :], final_norm[None, :])
    y_prompt = _layer(x_prompt, p_prompt[i], i, *args)
    y_sample = _layer(x_sample, p_sample[i], i, *args)
    return (y_prompt, y_sample)
```

```python
import functools
import math

import jax
import jax.numpy as jnp
import numpy as np
from jax import lax
from jax.experimental import pallas as pl
from jax.experimental.pallas import tpu as pltpu

F32 = jnp.float32
BF16 = jnp.bfloat16

D_MODEL = 2048
PLE_DIM = 256
ROPE_THETA = 10000.0
RMS_EPS = 1e-6
NEG = -1e30
LOG2E = math.log2(math.e)
LN2 = math.log(2.0)

DIFF_HEADS = 4
DIFF_QK_DIM = 64
DIFF_V_DIM = 128
DIFF_WIDTH = DIFF_HEADS * DIFF_V_DIM
DIL_PATTERNS = ((128, 1), (512, 4), (2048, 16))
DIL_HEADS = 4
DIL_HEAD_DIM = 128
DIL_WIDTH = DIL_HEADS * DIL_HEAD_DIM
DIL_RADIUS = 64
PERM_MIN_DILATION = 16
BAND_UNIT = 128
BAND_ROWS = 2048
assert all(w // (2 * d) == DIL_RADIUS for w, d in DIL_PATTERNS)

LANES = 128
MXU_DIM = 256
LSE_LANES = LANES // DIL_HEADS
BF16_SUBLANES = 16
VT_ROWS = DIFF_V_DIM + BF16_SUBLANES
VMEM_LIMIT_BYTES = 56 * 1024 * 1024


def _cparams(semantics):
    return pltpu.CompilerParams(dimension_semantics=semantics, vmem_limit_bytes=VMEM_LIMIT_BYTES)


def _rope_tables(tm, n_blocks):
    def layout(dim, reps):
        inv = np.float32(ROPE_THETA) ** (-np.arange(0, dim, 2, dtype=np.float32) / np.float32(dim))
        freq = np.tile(inv.astype(np.float64), reps)[None, :]
        base = (np.arange(n_blocks, dtype=np.float64) * tm)[:, None] * freq
        row = np.arange(tm, dtype=np.float64)[:, None] * freq
        blk = [jnp.asarray(f(base)[:, None, :], F32) for f in (np.cos, np.sin)]
        return blk + [jnp.asarray(f(row), F32) for f in (np.cos, np.sin)]

    return layout(DIFF_QK_DIM, LANES // (DIFF_QK_DIM // 2)), layout(DIL_HEAD_DIM, LANES // (DIL_HEAD_DIM // 2))


def _angle_add(cb_ref, sb_ref, cr_ref, sr_ref):
    cb, sb, cr, sr = cb_ref[...], sb_ref[...], cr_ref[...], sr_ref[...]
    return cb * cr - sb * sr, sb * cr + cb * sr


def _rope64(x, cos, sin_up, sin_dn):
    return x * cos + pltpu.roll(x, 96, 1) * sin_up + pltpu.roll(x, 32, 1) * sin_dn


def _rope128(x, cos, sin):
    return x * cos + pltpu.roll(x, 64, 1) * sin


def _normalize(x_ref, g_ref):
    x = x_ref[...]
    ms = jnp.mean(x * x, axis=-1, keepdims=True)
    return (x * lax.rsqrt(ms + RMS_EPS) * g_ref[...]).astype(BF16)


def _proj_diff_kernel(x_ref, g_ref, w_ref, cb_ref, sb_ref, cr_ref, sr_ref, qt_ref, k_ref, vt_ref, gate_ref):
    z = jnp.dot(_normalize(x_ref, g_ref), w_ref[...], preferred_element_type=F32)
    cos, sin = _angle_add(cb_ref, sb_ref, cr_ref, sr_ref)
    half = lax.broadcasted_iota(jnp.int32, (1, LANES), 1) % DIFF_QK_DIM < DIFF_QK_DIM // 2
    sup = sin * jnp.where(half, -1.0, 0.0)
    sdn = sin * jnp.where(half, 0.0, 1.0)
    qscale = DIFF_QK_DIM**-0.5 * LOG2E
    for h in range(DIFF_HEADS):
        lo = h * LANES
        q = _rope64(z[:, lo:lo + LANES], cos, sup, sdn) * qscale
        qt_ref[lo:lo + LANES, :] = q.T.astype(BF16)
        k = _rope64(z[:, DIFF_WIDTH + lo:DIFF_WIDTH + lo + LANES], cos, sup, sdn)
        k_ref[:, lo:lo + LANES] = k.astype(BF16)
        v = z[:, 2 * DIFF_WIDTH + lo:2 * DIFF_WIDTH + lo + LANES]
        vlo = h * VT_ROWS
        vt_ref[vlo:vlo + LANES, :] = v.T.astype(BF16)
        vt_ref[vlo + LANES:vlo + VT_ROWS, :] = jnp.ones((VT_ROWS - LANES, v.shape[0]), BF16)
    gate_ref[...] = z[:, 3 * DIFF_WIDTH:]


def _proj_dil_kernel(x_ref, g_ref, *refs, dilation, with_gate, use_perm):
    n_w = 4 if with_gate else 3
    w_refs, tab_refs, refs = refs[:n_w], refs[n_w:n_w + 4], refs[n_w + 4:]
    perm_ref = None
    if use_perm:
        perm_ref, refs = refs[0], refs[1:]
    q_ref, k_ref, v_ref = refs[:3]
    u = _normalize(x_ref, g_ref)
    zq, zk, zv = (jnp.dot(u, w[...], preferred_element_type=F32) for w in w_refs[:3])
    cos, sin = _angle_add(*tab_refs)
    first = lax.broadcasted_iota(jnp.int32, (1, LANES), 1) < DIL_HEAD_DIM // 2
    sin = sin * jnp.where(first, -1.0, 1.0)
    qscale = DIL_HEAD_DIM**-0.5 * LOG2E
    tm = u.shape[0]
    rows = tm // dilation
    if dilation == 1:
        for h in range(DIL_HEADS):
            lanes = slice(h * LANES, (h + 1) * LANES)
            q_ref[0, :, lanes] = (_rope128(zq[:, lanes], cos, sin) * qscale).astype(BF16)
            k_ref[0, :, lanes] = _rope128(zk[:, lanes], cos, sin).astype(BF16)
        v_ref[0] = zv.astype(BF16)
    elif perm_ref is not None:
        zb = jnp.concatenate(
            [jnp.concatenate([(_rope128(zq[:, h * LANES:(h + 1) * LANES], cos, sin) * qscale).astype(BF16)
                              for h in range(DIL_HEADS)], axis=1),
             jnp.concatenate([_rope128(zk[:, h * LANES:(h + 1) * LANES], cos, sin).astype(BF16)
                              for h in range(DIL_HEADS)], axis=1),
             zv.astype(BF16)], axis=1)
        part = perm_ref.shape[0]
        sub = part // dilation
        for p0 in range(0, tm, part):
            zp = jnp.dot(perm_ref[...], zb[p0:p0 + part], preferred_element_type=F32).astype(BF16)
            for r in range(dilation):
                for j, out_ref in enumerate((q_ref, k_ref, v_ref)):
                    out_ref[r, p0 // dilation:p0 // dilation + sub, :] = (
                        zp[r * sub:(r + 1) * sub, j * DIL_WIDTH:(j + 1) * DIL_WIDTH])
    else:
        stage_ref = refs[-1]
        for h in range(DIL_HEADS):
            lanes = slice(h * LANES, (h + 1) * LANES)
            stage_ref[h] = _rope128(zq[:, lanes], cos, sin) * qscale
            stage_ref[DIL_HEADS + h] = _rope128(zk[:, lanes], cos, sin)
            stage_ref[2 * DIL_HEADS + h] = zv[:, lanes]
        for r in range(dilation):
            for j, out_ref in enumerate((q_ref, k_ref, v_ref)):
                for h in range(DIL_HEADS):
                    blk = stage_ref[j * DIL_HEADS + h, pl.ds(r, rows, stride=dilation), :]
                    out_ref[r, :, h * LANES:(h + 1) * LANES] = blk.astype(BF16)
    if with_gate:
        refs[3][...] = jnp.dot(u, w_refs[3][...], preferred_element_type=F32)


def _proj_tile(seq):
    return min(512, seq)


def _rope_specs(tm):
    blk = pl.BlockSpec((None, 1, LANES), lambda b, i: (i, 0, 0))
    row = pl.BlockSpec((tm, LANES), lambda b, i: (0, 0))
    return [blk, blk, row, row]


def _proj_diff(x, g, w, tabs):
    batch, seq, _ = x.shape
    tm = _proj_tile(seq)
    row = lambda b, i: (b, i, 0)
    col = lambda b, i: (b, 0, i)
    return pl.pallas_call(
        _proj_diff_kernel,
        grid=(batch, seq // tm),
        in_specs=[
            pl.BlockSpec((None, tm, D_MODEL), row),
            pl.BlockSpec((1, D_MODEL), lambda b, i: (0, 0)),
            pl.BlockSpec((D_MODEL, 4 * DIFF_WIDTH), lambda b, i: (0, 0)),
            *_rope_specs(tm),
        ],
        out_specs=[
            pl.BlockSpec((None, DIFF_WIDTH, tm), col),
            pl.BlockSpec((None, tm, DIFF_WIDTH), row),
            pl.BlockSpec((None, DIFF_HEADS * VT_ROWS, tm), col),
            pl.BlockSpec((None, tm, DIFF_WIDTH), row),
        ],
        out_shape=[
            jax.ShapeDtypeStruct((batch, DIFF_WIDTH, seq), BF16),
            jax.ShapeDtypeStruct((batch, seq, DIFF_WIDTH), BF16),
            jax.ShapeDtypeStruct((batch, DIFF_HEADS * VT_ROWS, seq), BF16),
            jax.ShapeDtypeStruct((batch, seq, DIFF_WIDTH), F32),
        ],
        compiler_params=_cparams(("parallel", "parallel")),
        name="proj_diff",
    )(x, g, w, *tabs)


def _proj_dil(x, g, w, tabs, group, dilation, with_gate):
    batch, seq, _ = x.shape
    tm = _proj_tile(seq)
    rows = tm // dilation
    sub = seq // dilation
    row = lambda b, i: (b, i, 0)
    res = lambda b, i: (b, 0, i, 0)
    res_spec = pl.BlockSpec((None, dilation, rows, DIL_WIDTH), res)
    res_shape = jax.ShapeDtypeStruct((batch, dilation, sub, DIL_WIDTH), BF16)
    out_specs = [res_spec, res_spec, res_spec]
    out_shape = [res_shape, res_shape, res_shape]
    if with_gate:
        out_specs.append(pl.BlockSpec((None, tm, DIL_WIDTH), row))
        out_shape.append(jax.ShapeDtypeStruct((batch, seq, DIL_WIDTH), F32))
    use_perm = dilation >= PERM_MIN_DILATION
    scratch = [] if dilation == 1 or use_perm else [pltpu.VMEM((3 * DIL_HEADS, tm, LANES), F32)]
    perm_args, perm_specs = [], []
    if use_perm:
        part = min(MXU_DIM, tm)
        sub = part // dilation
        src = (np.arange(part) % sub) * dilation + np.arange(part) // sub
        perm_args = [jnp.asarray(np.eye(part, dtype=np.float32)[src], BF16)]
        perm_specs = [pl.BlockSpec((part, part), lambda b, i: (0, 0))]
    groups = len(DIL_PATTERNS)
    col_blocks = [4 + j * groups + group for j in range(3)] + ([4 + 3 * groups] if with_gate else [])
    w_specs = [pl.BlockSpec((D_MODEL, DIL_WIDTH), functools.partial(lambda b, i, c: (0, c), c=c))
               for c in col_blocks]
    return pl.pallas_call(
        functools.partial(_proj_dil_kernel, dilation=dilation, with_gate=with_gate, use_perm=use_perm),
        grid=(batch, seq // tm),
        in_specs=[
            pl.BlockSpec((None, tm, D_MODEL), row),
            pl.BlockSpec((1, D_MODEL), lambda b, i: (0, 0)),
            *w_specs,
            *_rope_specs(tm),
            *perm_specs,
        ],
        out_specs=out_specs,
        out_shape=out_shape,
        scratch_shapes=scratch,
        compiler_params=_cparams(("parallel", "parallel")),
        name=f"proj_dil{dilation}",
    )(x, g, *([w] * len(w_specs)), *tabs, *perm_args)


def _diff_attn_kernel(lam_ref, subln_ref, qt_ref, k_ref, vt_ref, gate_ref, o_ref,
                      w_scr, s_scr, bm_scr, m_scr, acc_scr, *, tk, unroll, peel, lam_init):
    nk = k_ref.shape[0] // tk
    qt = qt_ref[...]
    row = lax.broadcasted_iota(jnp.int32, qt.shape, 0)
    zero = jnp.zeros_like(qt)
    w_scr[0] = jnp.where(row < DIFF_QK_DIM, qt, zero)
    w_scr[1] = jnp.where(row >= DIFF_QK_DIM, qt, zero)
    m_scr[...] = jnp.full(m_scr.shape, NEG, F32)
    acc_scr[...] = jnp.zeros(acc_scr.shape, F32)

    def scores(j, slot):
        start = pl.multiple_of(j * tk, tk)
        kb = k_ref[pl.ds(start, tk), :]
        for c in range(2):
            s = jnp.dot(kb, w_scr[c], preferred_element_type=F32)
            s_scr[slot, c] = s
            bm_scr[slot, c] = jnp.max(s, axis=0, keepdims=True)

    def accumulate(j, slot):
        start = pl.multiple_of(j * tk, tk)
        vb = vt_ref[:, pl.ds(start, tk)]
        for c in range(2):
            m_old = m_scr[c]
            m_new = jnp.maximum(m_old, bm_scr[slot, c])
            alpha = jnp.exp2(m_old - m_new)
            p = jnp.exp2(s_scr[slot, c] - m_new).astype(BF16)
            acc_scr[c] = alpha * acc_scr[c] + jnp.dot(vb, p, preferred_element_type=F32)
            m_scr[c] = m_new

    scores(0, 0)

    def steps(j, last):
        for u in range(unroll):
            if last and u == unroll - 1:
                pass
            elif peel:
                scores(j + u + 1, (u + 1) % 2)
            else:
                scores(jnp.minimum(j + u + 1, nk - 1), (u + 1) % 2)
            accumulate(j + u, u % 2)

    def body(i, carry):
        steps(unroll * i, last=False)
        return carry

    if peel:
        lax.fori_loop(0, nk // unroll - 1, body, 0)
        steps(nk - unroll, last=True)
    else:
        lax.fori_loop(0, nk // unroll, body, 0)

    lam_p = lam_ref[...]
    lam = (jnp.exp(jnp.sum(lam_p[0:1] * lam_p[1:2], axis=-1, keepdims=True))
           - jnp.exp(jnp.sum(lam_p[2:3] * lam_p[3:4], axis=-1, keepdims=True)) + lam_init)
    a0, a1 = acc_scr[0], acc_scr[1]
    o_t = (a0[:DIFF_V_DIM] * (1.0 / a0[DIFF_V_DIM:DIFF_V_DIM + 1])
           - lam * (a1[:DIFF_V_DIM] * (1.0 / a1[DIFF_V_DIM:DIFF_V_DIM + 1])))
    ms = jnp.mean(o_t * o_t, axis=0, keepdims=True)
    o = (o_t * lax.rsqrt(ms + RMS_EPS)).T
    g = gate_ref[...]
    silu = g / (1.0 + jnp.exp(-g))
    o_ref[...] = (o * (subln_ref[...] * (1.0 - lam_init)) * silu).astype(o_ref.dtype)


def _diff_attention(lam_params, subln, qt, k, vt, gate, lam_init):
    batch, seq, _ = k.shape
    tq = min(512, seq)
    tk = min(512, seq // 2)
    unroll = next(u for u in (8, 4, 2) if seq // tk >= 2 * u)
    peel = seq // (unroll * tk) >= 3
    assert seq % (unroll * tk) == 0
    return pl.pallas_call(
        functools.partial(_diff_attn_kernel, tk=tk, unroll=unroll, peel=peel, lam_init=lam_init),
        grid=(batch, DIFF_HEADS, seq // tq),
        in_specs=[
            pl.BlockSpec(lam_params.shape, lambda b, h, i: (0, 0)),
            pl.BlockSpec(subln.shape, lambda b, h, i: (0, 0)),
            pl.BlockSpec((None, LANES, tq), lambda b, h, i: (b, h, i)),
            pl.BlockSpec((None, seq, LANES), lambda b, h, i: (b, 0, h)),
            pl.BlockSpec((None, VT_ROWS, seq), lambda b, h, i: (b, h, 0)),
            pl.BlockSpec((None, tq, LANES), lambda b, h, i: (b, i, h)),
        ],
        out_specs=pl.BlockSpec((None, tq, LANES), lambda b, h, i: (b, i, h)),
        out_shape=jax.ShapeDtypeStruct((batch, seq, DIFF_WIDTH), BF16),
        scratch_shapes=[
            pltpu.VMEM((2, LANES, tq), BF16),
            pltpu.VMEM((2, 2, tk, tq), F32),
            pltpu.VMEM((2, 2, 1, tq), F32),
            pltpu.VMEM((2, 1, tq), F32),
            pltpu.VMEM((2, VT_ROWS, tq), F32),
        ],
        compiler_params=_cparams(("parallel", "parallel", "parallel")),
        name="diff_attn",
    )(lam_params, subln, qt, k, vt, gate)


def _band_window(prev_ref, cur_ref, next_ref, unit, n_units, lo):
    r, q = DIL_RADIUS, BAND_UNIT
    lanes = slice(lo, lo + LANES)
    pieces = []
    if unit == 0:
        pieces.append(prev_ref[:, lanes])
    first = max(unit * q - r, 0)
    last = min(unit * q + q + r, n_units * q)
    pieces.append(cur_ref[first:last, lanes])
    if unit == n_units - 1:
        pieces.append(next_ref[:, lanes])
    return pieces[0] if len(pieces) == 1 else jnp.concatenate(pieces, axis=0)


def _band_attn_kernel(q_ref, kp_ref, kc_ref, kn_ref, vp_ref, vc_ref, vn_ref, o_ref, lse_ref, *, sub_len):
    t_rows = q_ref.shape[0]
    n_units = t_rows // BAND_UNIT
    period = min(sub_len, t_rows)
    tw = BAND_UNIT + 2 * DIL_RADIUS
    i = pl.program_id(0)
    row = lax.broadcasted_iota(jnp.int32, (BAND_UNIT, tw), 0)
    col = lax.broadcasted_iota(jnp.int32, (BAND_UNIT, tw), 1)
    rel = col - row
    band = jnp.where(rel >= 0, 0.0, NEG)
    band = jnp.where(rel <= 2 * DIL_RADIUS, band, NEG).astype(F32)
    for u in range(n_units):
        bias = band
        offset = (u * BAND_UNIT) % period
        may_start, may_end = offset == 0, offset == period - BAND_UNIT
        if may_start or may_end:
            kpos = lax.rem(i * t_rows + u * BAND_UNIT, sub_len) - DIL_RADIUS + col
            if may_start:
                bias = jnp.where(kpos >= 0, bias, NEG)
            if may_end:
                bias = jnp.where(kpos < sub_len, bias, NEG)
        rows = slice(u * BAND_UNIT, (u + 1) * BAND_UNIT)
        for h in range(DIL_HEADS):
            lo = h * LANES
            kwin = _band_window(kp_ref, kc_ref, kn_ref, u, n_units, lo)
            vwin = _band_window(vp_ref, vc_ref, vn_ref, u, n_units, lo)
            s = lax.dot_general(q_ref[rows, lo:lo + LANES], kwin,
                                (((1,), (1,)), ((), ())), preferred_element_type=F32)
            s = s + bias
            m = jnp.max(s, axis=-1, keepdims=True)
            p = jnp.exp2(s - m)
            den = jnp.sum(p, axis=-1, keepdims=True)
            o = jnp.dot(p.astype(BF16), vwin, preferred_element_type=F32)
            o_ref[rows, lo:lo + LANES] = (o * (1.0 / den)).astype(o_ref.dtype)
            lse = (m + jnp.log2(den)) * LN2
            lse_ref[rows, h * LSE_LANES:(h + 1) * LSE_LANES] = jnp.broadcast_to(lse, (BAND_UNIT, LSE_LANES))


def _band_attention(q, k, v, sub_len):
    n_rows = q.shape[0]
    t_rows = min(BAND_ROWS, n_rows)
    assert n_rows % t_rows == 0 and sub_len % BAND_UNIT == 0
    assert t_rows % sub_len == 0 or sub_len % t_rows == 0
    per = t_rows // DIL_RADIUS
    nhalo = n_rows // DIL_RADIUS
    cur = pl.BlockSpec((t_rows, DIL_WIDTH), lambda i: (i, 0))
    prev = pl.BlockSpec((DIL_RADIUS, DIL_WIDTH), lambda i: (jnp.maximum(i * per - 1, 0), 0))
    nxt = pl.BlockSpec((DIL_RADIUS, DIL_WIDTH), lambda i: (jnp.minimum((i + 1) * per, nhalo - 1), 0))
    return pl.pallas_call(
        functools.partial(_band_attn_kernel, sub_len=sub_len),
        grid=(n_rows // t_rows,),
        in_specs=[cur, prev, cur, nxt, prev, cur, nxt],
        out_specs=[cur, pl.BlockSpec((t_rows, LANES), lambda i: (i, 0))],
        out_shape=[jax.ShapeDtypeStruct((n_rows, DIL_WIDTH), BF16), jax.ShapeDtypeStruct((n_rows, LANES), F32)],
        compiler_params=_cparams(("parallel",)),
        name=f"band_attn{sub_len}",
    )(q, k, k, k, v, v, v)


def _rms(x, g):
    ms = jnp.mean(x * x, axis=-1, keepdims=True)
    return x * lax.rsqrt(ms + RMS_EPS) * g


def _row_parts(tm):
    part = tm // 2 if tm >= 2 * MXU_DIM else tm
    return [slice(r0, r0 + part) for r0 in range(0, tm, part)]


def _tail_kernel(x_ref, ya_ref, o0_ref, l0_ref, o1_ref, l1_ref, o2_ref, l2_ref, bg_ref, p_ref,
                 wout_ref, wgate_ref, wproj_ref, plen_ref, fnorm_ref, y_ref,
                 o1_scr, l1_scr, o2_scr, l2_scr, yb_scr):
    tm = x_ref.shape[0]
    for src, dst in ((o1_ref, o1_scr), (o2_ref, o2_scr)):
        d = src.shape[0]
        for r in range(d):
            for h in range(DIL_HEADS):
                dst[h, pl.ds(r, tm // d, stride=d), :] = src[r, :, h * LANES:(h + 1) * LANES].astype(F32)
    for src, dst in ((l1_ref, l1_scr), (l2_ref, l2_scr)):
        d = src.shape[0]
        for r in range(d):
            dst[pl.ds(r, tm // d, stride=d), :] = src[r]
    for rows in _row_parts(tm):
        n = rows.stop - rows.start
        l0, l1, l2 = l0_ref[rows, :], l1_scr[rows, :], l2_scr[rows, :]
        mx = jnp.maximum(jnp.maximum(l0, l1), l2)
        e0, e1, e2 = jnp.exp(l0 - mx), jnp.exp(l1 - mx), jnp.exp(l2 - mx)
        inv = 1.0 / (e0 + e1 + e2)
        w0, w1, w2 = e0 * inv, e1 * inv, e2 * inv
        for h in range(DIL_HEADS):
            slab = slice(h * LANES, (h + 1) * LANES)
            wide = lambda w: jnp.broadcast_to(w[:, h * LSE_LANES:h * LSE_LANES + 1], (n, LANES))
            ob = wide(w0) * o0_ref[rows, slab] + wide(w1) * o1_scr[h, rows, :] + wide(w2) * o2_scr[h, rows, :]
            bg = bg_ref[rows, slab]
            yb_scr[rows, slab] = (ob * (bg / (1.0 + jnp.exp(-bg)))).astype(BF16)
        hid = (x_ref[rows, :]
               + jnp.dot(ya_ref[rows, :], wout_ref[:DIFF_WIDTH, :], preferred_element_type=F32)
               + jnp.dot(yb_scr[rows, :], wout_ref[DIFF_WIDTH:, :], preferred_element_type=F32))
        nrm = _rms(hid, plen_ref[...]).astype(BF16)
        gate = 1.0 / (1.0 + jnp.exp(-jnp.dot(nrm, wgate_ref[...], preferred_element_type=F32)))
        hid = hid + gate * jnp.dot(p_ref[rows, :].astype(BF16), wproj_ref[...], preferred_element_type=F32)
        y_ref[rows, :] = _rms(hid, fnorm_ref[...])


def _tail(x, ya, o0, l0, o1, l1, o2, l2, bg, p, wout, wgate, wproj, plen, fnorm):
    batch, seq, _ = x.shape
    tm = min(512, seq)
    row = lambda b, i: (b, i, 0)
    full = lambda a: pl.BlockSpec(a.shape, lambda b, i: (0,) * a.ndim)
    wide = pl.BlockSpec((None, tm, D_MODEL), row)
    half = pl.BlockSpec((None, tm, DIL_WIDTH), row)
    lse = pl.BlockSpec((None, tm, LANES), row)

    def res(a):
        d, width = a.shape[1], a.shape[3]
        return pl.BlockSpec((None, d, tm // d, width), lambda b, i: (b, 0, i, 0))

    return pl.pallas_call(
        _tail_kernel,
        grid=(batch, seq // tm),
        in_specs=[wide, half, half, lse, res(o1), res(l1), res(o2), res(l2), half,
                  pl.BlockSpec((None, tm, PLE_DIM), row),
                  full(wout), full(wgate), full(wproj), full(plen), full(fnorm)],
        out_specs=wide,
        out_shape=jax.ShapeDtypeStruct((batch, seq, D_MODEL), F32),
        scratch_shapes=[pltpu.VMEM((DIL_HEADS, tm, LANES), F32), pltpu.VMEM((tm, LANES), F32),
                        pltpu.VMEM((DIL_HEADS, tm, LANES), F32), pltpu.VMEM((tm, LANES), F32),
                        pltpu.VMEM((tm, DIL_WIDTH), BF16)],
        compiler_params=_cparams(("parallel", "parallel")),
        name="tail",
    )(x, ya, o0, l0, o1, l1, o2, l2, bg, p, wout, wgate, wproj, plen, fnorm)


def _layer(x, p, layer_idx, tabs, norm_mix, w_in, lam_params, subln, wout, wgate, wproj, plen, fnorm):
    batch, seq, _ = x.shape
    tabs_a, tabs_b = tabs
    lam_init = 0.8 - 0.6 * math.exp(-0.3 * layer_idx)

    qt, k, vt, ag = _proj_diff(x, norm_mix, w_in, tabs_a)
    ya = _diff_attention(lam_params, subln, qt, k, vt, ag, lam_init)

    outs, lses, bg = [], [], None
    for g, (_, dilation) in enumerate(DIL_PATTERNS):
        res = _proj_dil(x, norm_mix, w_in, tabs_b, g, dilation, with_gate=(g == 0))
        if g == 0:
            bg = res[3]
        sub = seq // dilation
        q, kk, v = (a.reshape(batch * seq, DIL_WIDTH) for a in res[:3])
        o, lse = _band_attention(q, kk, v, sub)
        outs.append(o.reshape(batch, dilation, sub, DIL_WIDTH))
        lses.append(lse.reshape(batch, dilation, sub, LANES))

    return _tail(x, ya, outs[0].reshape(batch, seq, DIL_WIDTH), lses[0].reshape(batch, seq, LANES),
                 outs[1], lses[1], outs[2], lses[2], bg, p, wout, wgate, wproj, plen, fnorm)


def kernel(x_prompt, x_sample, p_prompt, p_sample, norm_mix, w_in, lam_q1, lam_k1, lam_q2, lam_k2,
           subln, w_out, ple_norm, w_ple_gate, w_ple_proj, final_norm):
    depth = w_in.shape[0]
    assert depth == 1, "the fused tail applies the final norm, so it handles a single layer"
    i = 0
    seqs = (x_prompt.shape[1], x_sample.shape[1])
    tm = _proj_tile(min(seqs))
    assert all(_proj_tile(n) == tm for n in seqs)
    tabs = _rope_tables(tm, max(seqs) // tm)
    lam_params = jnp.stack([lam_q1[i], lam_k1[i], lam_q2[i], lam_k2[i]], axis=0)
    args = (tabs, norm_mix[i][None, :], w_in[i].astype(BF16), lam_params, subln[i][None, :],
            w_out[i].astype(BF16), w_ple_gate[i].astype(BF16), w_ple_proj[i].astype(BF16),
            ple_norm[i][None, :], final_norm[None, :])
    y_prompt = _layer(x_prompt, p_prompt[i], i, *args)
    y_sample = _layer(x_sample, p_sample[i], i, *args)
    return (y_prompt, y_sample)
```

```python
import functools
import math

import jax
import jax.numpy as jnp
import numpy as np
from jax import lax
from jax.experimental import pallas as pl
from jax.experimental.pallas import tpu as pltpu

F32 = jnp.float32
BF16 = jnp.bfloat16

D_MODEL = 2048
PLE_DIM = 256
ROPE_THETA = 10000.0
RMS_EPS = 1e-6
NEG = -1e30
LOG2E = math.log2(math.e)
LN2 = math.log(2.0)

DIFF_HEADS = 4
DIFF_QK_DIM = 64
DIFF_V_DIM = 128
DIFF_WIDTH = DIFF_HEADS * DIFF_V_DIM
DIL_PATTERNS = ((128, 1), (512, 4), (2048, 16))
DIL_HEADS = 4
DIL_HEAD_DIM = 128
DIL_WIDTH = DIL_HEADS * DIL_HEAD_DIM
DIL_RADIUS = 64
PERM_MIN_DILATION = 16
BAND_UNIT = 128
BAND_ROWS = 2048
assert all(w // (2 * d) == DIL_RADIUS for w, d in DIL_PATTERNS)

LANES = 128
MXU_DIM = 256
LSE_LANES = LANES // DIL_HEADS
BF16_SUBLANES = 16
DIFF_Q_GROUP = 16
VT_ROWS = DIFF_V_DIM + BF16_SUBLANES
VMEM_LIMIT_BYTES = 56 * 1024 * 1024


def _cparams(semantics):
    return pltpu.CompilerParams(dimension_semantics=semantics, vmem_limit_bytes=VMEM_LIMIT_BYTES)


def _rope_tables(tm, n_blocks):
    def layout(dim, reps):
        inv = np.float32(ROPE_THETA) ** (-np.arange(0, dim, 2, dtype=np.float32) / np.float32(dim))
        freq = np.tile(inv.astype(np.float64), reps)[None, :]
        base = (np.arange(n_blocks, dtype=np.float64) * tm)[:, None] * freq
        row = np.arange(tm, dtype=np.float64)[:, None] * freq
        blk = [jnp.asarray(f(base)[:, None, :], F32) for f in (np.cos, np.sin)]
        return blk + [jnp.asarray(f(row), F32) for f in (np.cos, np.sin)]

    return layout(DIFF_QK_DIM, LANES // (DIFF_QK_DIM // 2)), layout(DIL_HEAD_DIM, LANES // (DIL_HEAD_DIM // 2))


def _angle_add(cb_ref, sb_ref, cr_ref, sr_ref):
    cb, sb, cr, sr = cb_ref[...], sb_ref[...], cr_ref[...], sr_ref[...]
    return cb * cr - sb * sr, sb * cr + cb * sr


def _rope64(x, cos, sin_up, sin_dn):
    return x * cos + pltpu.roll(x, 96, 1) * sin_up + pltpu.roll(x, 32, 1) * sin_dn


def _rope128(x, cos, sin):
    return x * cos + pltpu.roll(x, 64, 1) * sin


def _normalize(x_ref, g_ref):
    x = x_ref[...]
    ms = jnp.mean(x * x, axis=-1, keepdims=True)
    return (x * lax.rsqrt(ms + RMS_EPS) * g_ref[...]).astype(BF16)


def _proj_diff_kernel(x_ref, g_ref, w_ref, cb_ref, sb_ref, cr_ref, sr_ref, qt_ref, k_ref, vt_ref, gate_ref):
    z = jnp.dot(_normalize(x_ref, g_ref), w_ref[...], preferred_element_type=F32)
    cos, sin = _angle_add(cb_ref, sb_ref, cr_ref, sr_ref)
    half = lax.broadcasted_iota(jnp.int32, (1, LANES), 1) % DIFF_QK_DIM < DIFF_QK_DIM // 2
    sup = sin * jnp.where(half, -1.0, 0.0)
    sdn = sin * jnp.where(half, 0.0, 1.0)
    qscale = DIFF_QK_DIM**-0.5 * LOG2E
    for h in range(DIFF_HEADS):
        lo = h * LANES
        q = _rope64(z[:, lo:lo + LANES], cos, sup, sdn) * qscale
        qt_ref[lo:lo + LANES, :] = q.T.astype(BF16)
        k = _rope64(z[:, DIFF_WIDTH + lo:DIFF_WIDTH + lo + LANES], cos, sup, sdn)
        k_ref[:, lo:lo + LANES] = k.astype(BF16)
        v = z[:, 2 * DIFF_WIDTH + lo:2 * DIFF_WIDTH + lo + LANES]
        vlo = h * VT_ROWS
        vt_ref[vlo:vlo + LANES, :] = v.T.astype(BF16)
        vt_ref[vlo + LANES:vlo + VT_ROWS, :] = jnp.ones((VT_ROWS - LANES, v.shape[0]), BF16)
    gate_ref[...] = z[:, 3 * DIFF_WIDTH:]


def _proj_dil_kernel(x_ref, g_ref, *refs, dilation, with_gate, use_perm):
    n_w = 4 if with_gate else 3
    w_refs, tab_refs, refs = refs[:n_w], refs[n_w:n_w + 4], refs[n_w + 4:]
    perm_ref = None
    if use_perm:
        perm_ref, refs = refs[0], refs[1:]
    q_ref, k_ref, v_ref = refs[:3]
    u = _normalize(x_ref, g_ref)
    zq, zk, zv = (jnp.dot(u, w[...], preferred_element_type=F32) for w in w_refs[:3])
    cos, sin = _angle_add(*tab_refs)
    first = lax.broadcasted_iota(jnp.int32, (1, LANES), 1) < DIL_HEAD_DIM // 2
    sin = sin * jnp.where(first, -1.0, 1.0)
    qscale = DIL_HEAD_DIM**-0.5 * LOG2E
    tm = u.shape[0]
    rows = tm // dilation
    if dilation == 1:
        for h in range(DIL_HEADS):
            lanes = slice(h * LANES, (h + 1) * LANES)
            q_ref[0, :, lanes] = (_rope128(zq[:, lanes], cos, sin) * qscale).astype(BF16)
            k_ref[0, :, lanes] = _rope128(zk[:, lanes], cos, sin).astype(BF16)
        v_ref[0] = zv.astype(BF16)
    elif perm_ref is not None:
        zb = jnp.concatenate(
            [jnp.concatenate([(_rope128(zq[:, h * LANES:(h + 1) * LANES], cos, sin) * qscale).astype(BF16)
                              for h in range(DIL_HEADS)], axis=1),
             jnp.concatenate([_rope128(zk[:, h * LANES:(h + 1) * LANES], cos, sin).astype(BF16)
                              for h in range(DIL_HEADS)], axis=1),
             zv.astype(BF16)], axis=1)
        part = perm_ref.shape[0]
        sub = part // dilation
        for p0 in range(0, tm, part):
            zp = jnp.dot(perm_ref[...], zb[p0:p0 + part], preferred_element_type=F32).astype(BF16)
            for r in range(dilation):
                for j, out_ref in enumerate((q_ref, k_ref, v_ref)):
                    out_ref[r, p0 // dilation:p0 // dilation + sub, :] = (
                        zp[r * sub:(r + 1) * sub, j * DIL_WIDTH:(j + 1) * DIL_WIDTH])
    else:
        stage_ref = refs[-1]
        for h in range(DIL_HEADS):
            lanes = slice(h * LANES, (h + 1) * LANES)
            stage_ref[h] = _rope128(zq[:, lanes], cos, sin) * qscale
            stage_ref[DIL_HEADS + h] = _rope128(zk[:, lanes], cos, sin)
            stage_ref[2 * DIL_HEADS + h] = zv[:, lanes]
        for r in range(dilation):
            for j, out_ref in enumerate((q_ref, k_ref, v_ref)):
                for h in range(DIL_HEADS):
                    blk = stage_ref[j * DIL_HEADS + h, pl.ds(r, rows, stride=dilation), :]
                    out_ref[r, :, h * LANES:(h + 1) * LANES] = blk.astype(BF16)
    if with_gate:
        refs[3][...] = jnp.dot(u, w_refs[3][...], preferred_element_type=F32)


def _proj_tile(seq):
    return min(512, seq)


def _rope_specs(tm):
    blk = pl.BlockSpec((None, 1, LANES), lambda b, i: (i, 0, 0))
    row = pl.BlockSpec((tm, LANES), lambda b, i: (0, 0))
    return [blk, blk, row, row]


def _proj_diff(x, g, w, tabs):
    batch, seq, _ = x.shape
    tm = _proj_tile(seq)
    row = lambda b, i: (b, i, 0)
    col = lambda b, i: (b, 0, i)
    return pl.pallas_call(
        _proj_diff_kernel,
        grid=(batch, seq // tm),
        in_specs=[
            pl.BlockSpec((None, tm, D_MODEL), row),
            pl.BlockSpec((1, D_MODEL), lambda b, i: (0, 0)),
            pl.BlockSpec((D_MODEL, 4 * DIFF_WIDTH), lambda b, i: (0, 0)),
            *_rope_specs(tm),
        ],
        out_specs=[
            pl.BlockSpec((None, DIFF_WIDTH, tm), col),
            pl.BlockSpec((None, tm, DIFF_WIDTH), row),
            pl.BlockSpec((None, DIFF_HEADS * VT_ROWS, tm), col),
            pl.BlockSpec((None, tm, DIFF_WIDTH), row),
        ],
        out_shape=[
            jax.ShapeDtypeStruct((batch, DIFF_WIDTH, seq), BF16),
            jax.ShapeDtypeStruct((batch, seq, DIFF_WIDTH), BF16),
            jax.ShapeDtypeStruct((batch, DIFF_HEADS * VT_ROWS, seq), BF16),
            jax.ShapeDtypeStruct((batch, seq, DIFF_WIDTH), F32),
        ],
        compiler_params=_cparams(("parallel", "parallel")),
        name="proj_diff",
    )(x, g, w, *tabs)


def _proj_dil(x, g, w, tabs, group, dilation, with_gate):
    batch, seq, _ = x.shape
    tm = _proj_tile(seq)
    rows = tm // dilation
    sub = seq // dilation
    row = lambda b, i: (b, i, 0)
    res = lambda b, i: (b, 0, i, 0)
    res_spec = pl.BlockSpec((None, dilation, rows, DIL_WIDTH), res)
    res_shape = jax.ShapeDtypeStruct((batch, dilation, sub, DIL_WIDTH), BF16)
    out_specs = [res_spec, res_spec, res_spec]
    out_shape = [res_shape, res_shape, res_shape]
    if with_gate:
        out_specs.append(pl.BlockSpec((None, tm, DIL_WIDTH), row))
        out_shape.append(jax.ShapeDtypeStruct((batch, seq, DIL_WIDTH), F32))
    use_perm = dilation >= PERM_MIN_DILATION
    scratch = [] if dilation == 1 or use_perm else [pltpu.VMEM((3 * DIL_HEADS, tm, LANES), F32)]
    perm_args, perm_specs = [], []
    if use_perm:
        part = min(MXU_DIM, tm)
        sub = part // dilation
        src = (np.arange(part) % sub) * dilation + np.arange(part) // sub
        perm_args = [jnp.asarray(np.eye(part, dtype=np.float32)[src], BF16)]
        perm_specs = [pl.BlockSpec((part, part), lambda b, i: (0, 0))]
    groups = len(DIL_PATTERNS)
    col_blocks = [4 + j * groups + group for j in range(3)] + ([4 + 3 * groups] if with_gate else [])
    w_specs = [pl.BlockSpec((D_MODEL, DIL_WIDTH), functools.partial(lambda b, i, c: (0, c), c=c))
               for c in col_blocks]
    return pl.pallas_call(
        functools.partial(_proj_dil_kernel, dilation=dilation, with_gate=with_gate, use_perm=use_perm),
        grid=(batch, seq // tm),
        in_specs=[
            pl.BlockSpec((None, tm, D_MODEL), row),
            pl.BlockSpec((1, D_MODEL), lambda b, i: (0, 0)),
            *w_specs,
            *_rope_specs(tm),
            *perm_specs,
        ],
        out_specs=out_specs,
        out_shape=out_shape,
        scratch_shapes=scratch,
        compiler_params=_cparams(("parallel", "parallel")),
        name=f"proj_dil{dilation}",
    )(x, g, *([w] * len(w_specs)), *tabs, *perm_args)


def _diff_attn_kernel(lam_ref, subln_ref, qt_ref, k_ref, vt_ref, gate_ref, o_ref,
                      w_scr, wn_scr, s_scr, bm_scr, m_scr, acc_scr, fin_scr, *, tq, tk, unroll, lam_init):
    nk = k_ref.shape[0] // tk
    n_q = qt_ref.shape[1] // tq
    n_body = nk // unroll

    def load_w(dst, qi):
        qt = qt_ref[:, pl.ds(pl.multiple_of(qi * tq, tq), tq)]
        row = lax.broadcasted_iota(jnp.int32, qt.shape, 0)
        zero = jnp.zeros_like(qt)
        dst[0] = jnp.where(row < DIFF_QK_DIM, qt, zero)
        dst[1] = jnp.where(row >= DIFF_QK_DIM, qt, zero)

    def reset():
        m_scr[...] = jnp.full(m_scr.shape, NEG, F32)
        acc_scr[...] = jnp.zeros(acc_scr.shape, F32)

    def scores(w, j, slot):
        start = pl.multiple_of(j * tk, tk)
        kb = k_ref[pl.ds(start, tk), :]
        for c in range(2):
            s = jnp.dot(kb, w[c], preferred_element_type=F32)
            s_scr[slot, c] = s
            bm_scr[slot, c] = jnp.max(s, axis=0, keepdims=True)

    def accumulate(j, slot):
        start = pl.multiple_of(j * tk, tk)
        vb = vt_ref[:, pl.ds(start, tk)]
        for c in range(2):
            m_old = m_scr[c]
            m_new = jnp.maximum(m_old, bm_scr[slot, c])
            alpha = jnp.exp2(m_old - m_new)
            p = jnp.exp2(s_scr[slot, c] - m_new).astype(BF16)
            acc_scr[c] = alpha * acc_scr[c] + jnp.dot(vb, p, preferred_element_type=F32)
            m_scr[c] = m_new

    def steps(j, to_next_query_block):
        for u in range(unroll):
            if to_next_query_block and u == unroll - 1:
                scores(wn_scr, 0, 0)
            else:
                scores(w_scr, j + u + 1, (u + 1) % 2)
            accumulate(j + u, u % 2)

    def output(qi):
        rows = pl.ds(pl.multiple_of(qi * tq, tq), tq)
        lam_p = lam_ref[...]
        lam = (jnp.exp(jnp.sum(lam_p[0:1] * lam_p[1:2], axis=-1, keepdims=True))
               - jnp.exp(jnp.sum(lam_p[2:3] * lam_p[3:4], axis=-1, keepdims=True)) + lam_init)
        a0, a1 = fin_scr[0], fin_scr[1]
        o_t = (a0[:DIFF_V_DIM] * (1.0 / a0[DIFF_V_DIM:DIFF_V_DIM + 1])
               - lam * (a1[:DIFF_V_DIM] * (1.0 / a1[DIFF_V_DIM:DIFF_V_DIM + 1])))
        ms = jnp.mean(o_t * o_t, axis=0, keepdims=True)
        o = (o_t * lax.rsqrt(ms + RMS_EPS)).T
        g = gate_ref[rows, :]
        silu = g / (1.0 + jnp.exp(-g))
        o_ref[rows, :] = (o * (subln_ref[...] * (1.0 - lam_init)) * silu).astype(o_ref.dtype)

    load_w(w_scr, 0)
    reset()
    fin_scr[...] = jnp.ones(fin_scr.shape, F32)
    scores(w_scr, 0, 0)

    def query_block(qi, carry):
        output(jnp.maximum(qi - 1, 0))
        steps(0, False)
        if n_body > 2:
            def body(i, c):
                steps(unroll * i, False)
                return c
            lax.fori_loop(1, n_body - 1, body, 0)
        load_w(wn_scr, jnp.minimum(qi + 1, n_q - 1))
        steps(nk - unroll, True)
        fin_scr[...] = acc_scr[...]
        reset()
        w_scr[...] = wn_scr[...]
        return carry

    lax.fori_loop(0, n_q, query_block, 0)
    output(n_q - 1)


def _diff_attention(lam_params, subln, qt, k, vt, gate, lam_init):
    batch, seq, _ = k.shape
    tq = min(512, seq)
    tk = min(512, seq // 2)
    unroll = next(u for u in (8, 4, 2) if seq // tk >= 2 * u)
    assert seq % (unroll * tk) == 0 and seq // (unroll * tk) >= 2
    tqg = min(DIFF_Q_GROUP * tq, seq)
    return pl.pallas_call(
        functools.partial(_diff_attn_kernel, tq=tq, tk=tk, unroll=unroll, lam_init=lam_init),
        grid=(batch, DIFF_HEADS, seq // tqg),
        in_specs=[
            pl.BlockSpec(lam_params.shape, lambda b, h, i: (0, 0)),
            pl.BlockSpec(subln.shape, lambda b, h, i: (0, 0)),
            pl.BlockSpec((None, LANES, tqg), lambda b, h, i: (b, h, i)),
            pl.BlockSpec((None, seq, LANES), lambda b, h, i: (b, 0, h)),
            pl.BlockSpec((None, VT_ROWS, seq), lambda b, h, i: (b, h, 0)),
            pl.BlockSpec((None, tqg, LANES), lambda b, h, i: (b, i, h)),
        ],
        out_specs=pl.BlockSpec((None, tqg, LANES), lambda b, h, i: (b, i, h)),
        out_shape=jax.ShapeDtypeStruct((batch, seq, DIFF_WIDTH), BF16),
        scratch_shapes=[
            pltpu.VMEM((2, LANES, tq), BF16),
            pltpu.VMEM((2, LANES, tq), BF16),
            pltpu.VMEM((2, 2, tk, tq), F32),
            pltpu.VMEM((2, 2, 1, tq), F32),
            pltpu.VMEM((2, 1, tq), F32),
            pltpu.VMEM((2, VT_ROWS, tq), F32),
            pltpu.VMEM((2, VT_ROWS, tq), F32),
        ],
        compiler_params=_cparams(("parallel", "parallel", "parallel")),
        name="diff_attn",
    )(lam_params, subln, qt, k, vt, gate)


def _band_window(prev_ref, cur_ref, next_ref, unit, n_units, lo):
    r, q = DIL_RADIUS, BAND_UNIT
    lanes = slice(lo, lo + LANES)
    pieces = []
    if unit == 0:
        pieces.append(prev_ref[:, lanes])
    first = max(unit * q - r, 0)
    last = min(unit * q + q + r, n_units * q)
    pieces.append(cur_ref[first:last, lanes])
    if unit == n_units - 1:
        pieces.append(next_ref[:, lanes])
    return pieces[0] if len(pieces) == 1 else jnp.concatenate(pieces, axis=0)


def _band_attn_kernel(q_ref, kp_ref, kc_ref, kn_ref, vp_ref, vc_ref, vn_ref, o_ref, lse_ref, *, sub_len):
    t_rows = q_ref.shape[0]
    n_units = t_rows // BAND_UNIT
    period = min(sub_len, t_rows)
    tw = BAND_UNIT + 2 * DIL_RADIUS
    i = pl.program_id(0)
    row = lax.broadcasted_iota(jnp.int32, (BAND_UNIT, tw), 0)
    col = lax.broadcasted_iota(jnp.int32, (BAND_UNIT, tw), 1)
    rel = col - row
    band = jnp.where(rel >= 0, 0.0, NEG)
    band = jnp.where(rel <= 2 * DIL_RADIUS, band, NEG).astype(F32)
    for u in range(n_units):
        bias = band
        offset = (u * BAND_UNIT) % period
        may_start, may_end = offset == 0, offset == period - BAND_UNIT
        if may_start or may_end:
            kpos = lax.rem(i * t_rows + u * BAND_UNIT, sub_len) - DIL_RADIUS + col
            if may_start:
                bias = jnp.where(kpos >= 0, bias, NEG)
            if may_end:
                bias = jnp.where(kpos < sub_len, bias, NEG)
        rows = slice(u * BAND_UNIT, (u + 1) * BAND_UNIT)
        for h in range(DIL_HEADS):
            lo = h * LANES
            kwin = _band_window(kp_ref, kc_ref, kn_ref, u, n_units, lo)
            vwin = _band_window(vp_ref, vc_ref, vn_ref, u, n_units, lo)
            s = lax.dot_general(q_ref[rows, lo:lo + LANES], kwin,
                                (((1,), (1,)), ((), ())), preferred_element_type=F32)
            s = s + bias
            m = jnp.max(s, axis=-1, keepdims=True)
            p = jnp.exp2(s - m)
            den = jnp.sum(p, axis=-1, keepdims=True)
            o = jnp.dot(p.astype(BF16), vwin, preferred_element_type=F32)
            o_ref[rows, lo:lo + LANES] = (o * (1.0 / den)).astype(o_ref.dtype)
            lse = (m + jnp.log2(den)) * LN2
            lse_ref[rows, h * LSE_LANES:(h + 1) * LSE_LANES] = jnp.broadcast_to(lse, (BAND_UNIT, LSE_LANES))


def _band_attention(q, k, v, sub_len):
    n_rows = q.shape[0]
    t_rows = min(BAND_ROWS, n_rows)
    assert n_rows % t_rows == 0 and sub_len % BAND_UNIT == 0
    assert t_rows % sub_len == 0 or sub_len % t_rows == 0
    per = t_rows // DIL_RADIUS
    nhalo = n_rows // DIL_RADIUS
    cur = pl.BlockSpec((t_rows, DIL_WIDTH), lambda i: (i, 0))
    prev = pl.BlockSpec((DIL_RADIUS, DIL_WIDTH), lambda i: (jnp.maximum(i * per - 1, 0), 0))
    nxt = pl.BlockSpec((DIL_RADIUS, DIL_WIDTH), lambda i: (jnp.minimum((i + 1) * per, nhalo - 1), 0))
    return pl.pallas_call(
        functools.partial(_band_attn_kernel, sub_len=sub_len),
        grid=(n_rows // t_rows,),
        in_specs=[cur, prev, cur, nxt, prev, cur, nxt],
        out_specs=[cur, pl.BlockSpec((t_rows, LANES), lambda i: (i, 0))],
        out_shape=[jax.ShapeDtypeStruct((n_rows, DIL_WIDTH), BF16), jax.ShapeDtypeStruct((n_rows, LANES), F32)],
        compiler_params=_cparams(("parallel",)),
        name=f"band_attn{sub_len}",
    )(q, k, k, k, v, v, v)


def _rms(x, g):
    ms = jnp.mean(x * x, axis=-1, keepdims=True)
    return x * lax.rsqrt(ms + RMS_EPS) * g


def _row_parts(tm):
    part = tm // 2 if tm >= 2 * MXU_DIM else tm
    return [slice(r0, r0 + part) for r0 in range(0, tm, part)]


def _tail_kernel(x_ref, ya_ref, o0_ref, l0_ref, o1_ref, l1_ref, o2_ref, l2_ref, bg_ref, p_ref,
                 wout_ref, wgate_ref, wproj_ref, plen_ref, fnorm_ref, y_ref,
                 o1_scr, l1_scr, o2_scr, l2_scr, yb_scr):
    tm = x_ref.shape[0]
    for src, dst in ((o1_ref, o1_scr), (o2_ref, o2_scr)):
        d = src.shape[0]
        for r in range(d):
            for h in range(DIL_HEADS):
                dst[h, pl.ds(r, tm // d, stride=d), :] = src[r, :, h * LANES:(h + 1) * LANES].astype(F32)
    for src, dst in ((l1_ref, l1_scr), (l2_ref, l2_scr)):
        d = src.shape[0]
        for r in range(d):
            dst[pl.ds(r, tm // d, stride=d), :] = src[r]
    for rows in _row_parts(tm):
        n = rows.stop - rows.start
        l0, l1, l2 = l0_ref[rows, :], l1_scr[rows, :], l2_scr[rows, :]
        mx = jnp.maximum(jnp.maximum(l0, l1), l2)
        e0, e1, e2 = jnp.exp(l0 - mx), jnp.exp(l1 - mx), jnp.exp(l2 - mx)
        inv = 1.0 / (e0 + e1 + e2)
        w0, w1, w2 = e0 * inv, e1 * inv, e2 * inv
        for h in range(DIL_HEADS):
            slab = slice(h * LANES, (h + 1) * LANES)
            wide = lambda w: jnp.broadcast_to(w[:, h * LSE_LANES:h * LSE_LANES + 1], (n, LANES))
            ob = wide(w0) * o0_ref[rows, slab] + wide(w1) * o1_scr[h, rows, :] + wide(w2) * o2_scr[h, rows, :]
            bg = bg_ref[rows, slab]
            yb_scr[rows, slab] = (ob * (bg / (1.0 + jnp.exp(-bg)))).astype(BF16)
        hid = (x_ref[rows, :]
               + jnp.dot(ya_ref[rows, :], wout_ref[:DIFF_WIDTH, :], preferred_element_type=F32)
               + jnp.dot(yb_scr[rows, :], wout_ref[DIFF_WIDTH:, :], preferred_element_type=F32))
        nrm = _rms(hid, plen_ref[...]).astype(BF16)
        gate = 1.0 / (1.0 + jnp.exp(-jnp.dot(nrm, wgate_ref[...], preferred_element_type=F32)))
        hid = hid + gate * jnp.dot(p_ref[rows, :].astype(BF16), wproj_ref[...], preferred_element_type=F32)
        y_ref[rows, :] = _rms(hid, fnorm_ref[...])


def _tail(x, ya, o0, l0, o1, l1, o2, l2, bg, p, wout, wgate, wproj, plen, fnorm):
    batch, seq, _ = x.shape
    tm = min(512, seq)
    row = lambda b, i: (b, i, 0)
    full = lambda a: pl.BlockSpec(a.shape, lambda b, i: (0,) * a.ndim)
    wide = pl.BlockSpec((None, tm, D_MODEL), row)
    half = pl.BlockSpec((None, tm, DIL_WIDTH), row)
    lse = pl.BlockSpec((None, tm, LANES), row)

    def res(a):
        d, width = a.shape[1], a.shape[3]
        return pl.BlockSpec((None, d, tm // d, width), lambda b, i: (b, 0, i, 0))

    return pl.pallas_call(
        _tail_kernel,
        grid=(batch, seq // tm),
        in_specs=[wide, half, half, lse, res(o1), res(l1), res(o2), res(l2), half,
                  pl.BlockSpec((None, tm, PLE_DIM), row),
                  full(wout), full(wgate), full(wproj), full(plen), full(fnorm)],
        out_specs=wide,
        out_shape=jax.ShapeDtypeStruct((batch, seq, D_MODEL), F32),
        scratch_shapes=[pltpu.VMEM((DIL_HEADS, tm, LANES), F32), pltpu.VMEM((tm, LANES), F32),
                        pltpu.VMEM((DIL_HEADS, tm, LANES), F32), pltpu.VMEM((tm, LANES), F32),
                        pltpu.VMEM((tm, DIL_WIDTH), BF16)],
        compiler_params=_cparams(("parallel", "parallel")),
        name="tail",
    )(x, ya, o0, l0, o1, l1, o2, l2, bg, p, wout, wgate, wproj, plen, fnorm)


def _layer(x, p, layer_idx, tabs, norm_mix, w_in, lam_params, subln, wout, wgate, wproj, plen, fnorm):
    batch, seq, _ = x.shape
    tabs_a, tabs_b = tabs
    lam_init = 0.8 - 0.6 * math.exp(-0.3 * layer_idx)

    qt, k, vt, ag = _proj_diff(x, norm_mix, w_in, tabs_a)
    ya = _diff_attention(lam_params, subln, qt, k, vt, ag, lam_init)

    outs, lses, bg = [], [], None
    for g, (_, dilation) in enumerate(DIL_PATTERNS):
        res = _proj_dil(x, norm_mix, w_in, tabs_b, g, dilation, with_gate=(g == 0))
        if g == 0:
            bg = res[3]
        sub = seq // dilation
        q, kk, v = (a.reshape(batch * seq, DIL_WIDTH) for a in res[:3])
        o, lse = _band_attention(q, kk, v, sub)
        outs.append(o.reshape(batch, dilation, sub, DIL_WIDTH))
        lses.append(lse.reshape(batch, dilation, sub, LANES))

    return _tail(x, ya, outs[0].reshape(batch, seq, DIL_WIDTH), lses[0].reshape(batch, seq, LANES),
                 outs[1], lses[1], outs[2], lses[2], bg, p, wout, wgate, wproj, plen, fnorm)


def kernel(x_prompt, x_sample, p_prompt, p_sample, norm_mix, w_in, lam_q1, lam_k1, lam_q2, lam_k2,
           subln, w_out, ple_norm, w_ple_gate, w_ple_proj, final_norm):
    depth = w_in.shape[0]
    assert depth == 1, "the fused tail applies the final norm, so it handles a single layer"
    i = 0
    seqs = (x_prompt.shape[1], x_sample.shape[1])
    tm = _proj_tile(min(seqs))
    assert all(_proj_tile(n) == tm for n in seqs)
    tabs = _rope_tables(tm, max(seqs) // tm)
    lam_params = jnp.stack([lam_q1[i], lam_k1[i], lam_q2[i], lam_k2[i]], axis=0)
    args = (tabs, norm_mix[i][None, :], w_in[i].astype(BF16), lam_params, subln[i][None, :],
            w_out[i].astype(BF16), w_ple_gate[i].astype(BF16), w_ple_proj[i].astype(BF16),
            ple_norm[i][None, :], final_norm[None, :])
    y_prompt = _layer(x_prompt, p_prompt[i], i, *args)
    y_sample = _layer(x_sample, p_sample[i], i, *args)
    return (y_prompt, y_sample)
```

```python
import functools
import math

import jax
import jax.numpy as jnp
import numpy as np
from jax import lax
from jax.experimental import pallas as pl
from jax.experimental.pallas import tpu as pltpu

F32 = jnp.float32
BF16 = jnp.bfloat16

D_MODEL = 2048
PLE_DIM = 256
ROPE_THETA = 10000.0
RMS_EPS = 1e-6
NEG = -1e30
LOG2E = math.log2(math.e)
LN2 = math.log(2.0)

DIFF_HEADS = 4
DIFF_QK_DIM = 64
DIFF_V_DIM = 128
DIFF_WIDTH = DIFF_HEADS * DIFF_V_DIM
DIL_PATTERNS = ((128, 1), (512, 4), (2048, 16))
DIL_HEADS = 4
DIL_HEAD_DIM = 128
DIL_WIDTH = DIL_HEADS * DIL_HEAD_DIM
DIL_RADIUS = 64
PERM_MIN_DILATION = 16
BAND_UNIT = 128
BAND_ROWS = 4096
ROW_TILE = 512
ATTN_BLOCK = 512
assert all(w // (2 * d) == DIL_RADIUS for w, d in DIL_PATTERNS)

LANES = 128
MXU_DIM = 256
LSE_LANES = LANES // DIL_HEADS
BF16_SUBLANES = 16
DIFF_Q_GROUP = 16
VT_ROWS = DIFF_V_DIM + BF16_SUBLANES
VMEM_LIMIT_BYTES = 56 * 1024 * 1024


def _cparams(semantics):
    return pltpu.CompilerParams(dimension_semantics=semantics, vmem_limit_bytes=VMEM_LIMIT_BYTES)


def _rope_tables(tm, n_blocks):
    def layout(dim, reps):
        inv = np.float32(ROPE_THETA) ** (-np.arange(0, dim, 2, dtype=np.float32) / np.float32(dim))
        freq = np.tile(inv.astype(np.float64), reps)[None, :]
        base = (np.arange(n_blocks, dtype=np.float64) * tm)[:, None] * freq
        row = np.arange(tm, dtype=np.float64)[:, None] * freq
        blk = [jnp.asarray(f(base)[:, None, :], F32) for f in (np.cos, np.sin)]
        return blk + [jnp.asarray(f(row), F32) for f in (np.cos, np.sin)]

    return layout(DIFF_QK_DIM, LANES // (DIFF_QK_DIM // 2)), layout(DIL_HEAD_DIM, LANES // (DIL_HEAD_DIM // 2))


def _angle_add(cb_ref, sb_ref, cr_ref, sr_ref):
    cb, sb, cr, sr = cb_ref[...], sb_ref[...], cr_ref[...], sr_ref[...]
    return cb * cr - sb * sr, sb * cr + cb * sr


def _rope64(x, cos, sin_up, sin_dn):
    return x * cos + pltpu.roll(x, 96, 1) * sin_up + pltpu.roll(x, 32, 1) * sin_dn


def _rope128(x, cos, sin):
    return x * cos + pltpu.roll(x, 64, 1) * sin


def _normalize(x_ref, g_ref):
    x = x_ref[...]
    ms = jnp.mean(x * x, axis=-1, keepdims=True)
    return (x * lax.rsqrt(ms + RMS_EPS) * g_ref[...]).astype(BF16)


def _proj_diff_kernel(x_ref, g_ref, w_ref, cb_ref, sb_ref, cr_ref, sr_ref, qt_ref, k_ref, vt_ref, gate_ref):
    z = jnp.dot(_normalize(x_ref, g_ref), w_ref[...], preferred_element_type=F32)
    cos, sin = _angle_add(cb_ref, sb_ref, cr_ref, sr_ref)
    half = lax.broadcasted_iota(jnp.int32, (1, LANES), 1) % DIFF_QK_DIM < DIFF_QK_DIM // 2
    sup = sin * jnp.where(half, -1.0, 0.0)
    sdn = sin * jnp.where(half, 0.0, 1.0)
    qscale = DIFF_QK_DIM**-0.5 * LOG2E
    for h in range(DIFF_HEADS):
        lo = h * LANES
        q = _rope64(z[:, lo:lo + LANES], cos, sup, sdn) * qscale
        qt_ref[lo:lo + LANES, :] = q.T.astype(BF16)
        k = _rope64(z[:, DIFF_WIDTH + lo:DIFF_WIDTH + lo + LANES], cos, sup, sdn)
        k_ref[:, lo:lo + LANES] = k.astype(BF16)
        v = z[:, 2 * DIFF_WIDTH + lo:2 * DIFF_WIDTH + lo + LANES]
        vlo = h * VT_ROWS
        vt_ref[vlo:vlo + LANES, :] = v.T.astype(BF16)
        vt_ref[vlo + LANES:vlo + VT_ROWS, :] = jnp.ones((VT_ROWS - LANES, v.shape[0]), BF16)
    gate_ref[...] = z[:, 3 * DIFF_WIDTH:]


def _proj_dil_kernel(x_ref, g_ref, *refs, dilation, with_gate, use_perm):
    n_w = 4 if with_gate else 3
    w_refs, tab_refs, refs = refs[:n_w], refs[n_w:n_w + 4], refs[n_w + 4:]
    perm_ref = None
    if use_perm:
        perm_ref, refs = refs[0], refs[1:]
    q_ref, k_ref, v_ref = refs[:3]
    u = _normalize(x_ref, g_ref)
    zq, zk, zv = (jnp.dot(u, w[...], preferred_element_type=F32) for w in w_refs[:3])
    cos, sin = _angle_add(*tab_refs)
    first = lax.broadcasted_iota(jnp.int32, (1, LANES), 1) < DIL_HEAD_DIM // 2
    sin = sin * jnp.where(first, -1.0, 1.0)
    qscale = DIL_HEAD_DIM**-0.5 * LOG2E
    tm = u.shape[0]
    rows = tm // dilation
    if dilation == 1:
        for h in range(DIL_HEADS):
            lanes = slice(h * LANES, (h + 1) * LANES)
            q_ref[0, :, lanes] = (_rope128(zq[:, lanes], cos, sin) * qscale).astype(BF16)
            k_ref[0, :, lanes] = _rope128(zk[:, lanes], cos, sin).astype(BF16)
        v_ref[0] = zv.astype(BF16)
    elif perm_ref is not None:
        zb = jnp.concatenate(
            [jnp.concatenate([(_rope128(zq[:, h * LANES:(h + 1) * LANES], cos, sin) * qscale).astype(BF16)
                              for h in range(DIL_HEADS)], axis=1),
             jnp.concatenate([_rope128(zk[:, h * LANES:(h + 1) * LANES], cos, sin).astype(BF16)
                              for h in range(DIL_HEADS)], axis=1),
             zv.astype(BF16)], axis=1)
        part = perm_ref.shape[0]
        sub = part // dilation
        for p0 in range(0, tm, part):
            zp = jnp.dot(perm_ref[...], zb[p0:p0 + part], preferred_element_type=F32).astype(BF16)
            for r in range(dilation):
                for j, out_ref in enumerate((q_ref, k_ref, v_ref)):
                    out_ref[r, p0 // dilation:p0 // dilation + sub, :] = (
                        zp[r * sub:(r + 1) * sub, j * DIL_WIDTH:(j + 1) * DIL_WIDTH])
    else:
        stage_ref = refs[-1]
        for h in range(DIL_HEADS):
            lanes = slice(h * LANES, (h + 1) * LANES)
            stage_ref[h] = _rope128(zq[:, lanes], cos, sin) * qscale
            stage_ref[DIL_HEADS + h] = _rope128(zk[:, lanes], cos, sin)
            stage_ref[2 * DIL_HEADS + h] = zv[:, lanes]
        for r in range(dilation):
            for j, out_ref in enumerate((q_ref, k_ref, v_ref)):
                for h in range(DIL_HEADS):
                    blk = stage_ref[j * DIL_HEADS + h, pl.ds(r, rows, stride=dilation), :]
                    out_ref[r, :, h * LANES:(h + 1) * LANES] = blk.astype(BF16)
    if with_gate:
        refs[3][...] = jnp.dot(u, w_refs[3][...], preferred_element_type=F32)


def _proj_tile(seq):
    return min(ROW_TILE, seq)


def _rope_specs(tm):
    blk = pl.BlockSpec((None, 1, LANES), lambda b, i: (i, 0, 0))
    row = pl.BlockSpec((tm, LANES), lambda b, i: (0, 0))
    return [blk, blk, row, row]


def _proj_diff(x, g, w, tabs):
    batch, seq, _ = x.shape
    tm = _proj_tile(seq)
    row = lambda b, i: (b, i, 0)
    col = lambda b, i: (b, 0, i)
    return pl.pallas_call(
        _proj_diff_kernel,
        grid=(batch, seq // tm),
        in_specs=[
            pl.BlockSpec((None, tm, D_MODEL), row),
            pl.BlockSpec((1, D_MODEL), lambda b, i: (0, 0)),
            pl.BlockSpec((D_MODEL, 4 * DIFF_WIDTH), lambda b, i: (0, 0)),
            *_rope_specs(tm),
        ],
        out_specs=[
            pl.BlockSpec((None, DIFF_WIDTH, tm), col),
            pl.BlockSpec((None, tm, DIFF_WIDTH), row),
            pl.BlockSpec((None, DIFF_HEADS * VT_ROWS, tm), col),
            pl.BlockSpec((None, tm, DIFF_WIDTH), row),
        ],
        out_shape=[
            jax.ShapeDtypeStruct((batch, DIFF_WIDTH, seq), BF16),
            jax.ShapeDtypeStruct((batch, seq, DIFF_WIDTH), BF16),
            jax.ShapeDtypeStruct((batch, DIFF_HEADS * VT_ROWS, seq), BF16),
            jax.ShapeDtypeStruct((batch, seq, DIFF_WIDTH), F32),
        ],
        compiler_params=_cparams(("parallel", "parallel")),
        name="proj_diff",
    )(x, g, w, *tabs)


def _proj_dil(x, g, w, tabs, group, dilation, with_gate):
    batch, seq, _ = x.shape
    tm = _proj_tile(seq)
    rows = tm // dilation
    sub = seq // dilation
    row = lambda b, i: (b, i, 0)
    res = lambda b, i: (b, 0, i, 0)
    res_spec = pl.BlockSpec((None, dilation, rows, DIL_WIDTH), res)
    res_shape = jax.ShapeDtypeStruct((batch, dilation, sub, DIL_WIDTH), BF16)
    out_specs = [res_spec, res_spec, res_spec]
    out_shape = [res_shape, res_shape, res_shape]
    if with_gate:
        out_specs.append(pl.BlockSpec((None, tm, DIL_WIDTH), row))
        out_shape.append(jax.ShapeDtypeStruct((batch, seq, DIL_WIDTH), F32))
    use_perm = dilation >= PERM_MIN_DILATION
    scratch = [] if dilation == 1 or use_perm else [pltpu.VMEM((3 * DIL_HEADS, tm, LANES), F32)]
    perm_args, perm_specs = [], []
    if use_perm:
        part = min(MXU_DIM, tm)
        sub = part // dilation
        src = (np.arange(part) % sub) * dilation + np.arange(part) // sub
        perm_args = [jnp.asarray(np.eye(part, dtype=np.float32)[src], BF16)]
        perm_specs = [pl.BlockSpec((part, part), lambda b, i: (0, 0))]
    groups = len(DIL_PATTERNS)
    col_blocks = [4 + j * groups + group for j in range(3)] + ([4 + 3 * groups] if with_gate else [])
    w_specs = [pl.BlockSpec((D_MODEL, DIL_WIDTH), functools.partial(lambda b, i, c: (0, c), c=c))
               for c in col_blocks]
    return pl.pallas_call(
        functools.partial(_proj_dil_kernel, dilation=dilation, with_gate=with_gate, use_perm=use_perm),
        grid=(batch, seq // tm),
        in_specs=[
            pl.BlockSpec((None, tm, D_MODEL), row),
            pl.BlockSpec((1, D_MODEL), lambda b, i: (0, 0)),
            *w_specs,
            *_rope_specs(tm),
            *perm_specs,
        ],
        out_specs=out_specs,
        out_shape=out_shape,
        scratch_shapes=scratch,
        compiler_params=_cparams(("parallel", "parallel")),
        name=f"proj_dil{dilation}",
    )(x, g, *([w] * len(w_specs)), *tabs, *perm_args)


def _diff_attn_kernel(lam_ref, subln_ref, qt_ref, k_ref, vt_ref, gate_ref, o_ref,
                      w_scr, wn_scr, s_scr, bm_scr, m_scr, acc_scr, fin_scr, *, tq, tk, unroll, lam_init):
    nk = k_ref.shape[0] // tk
    n_q = qt_ref.shape[1] // tq
    n_body = nk // unroll

    def load_w(dst, qi):
        qt = qt_ref[:, pl.ds(pl.multiple_of(qi * tq, tq), tq)]
        row = lax.broadcasted_iota(jnp.int32, qt.shape, 0)
        zero = jnp.zeros_like(qt)
        dst[0] = jnp.where(row < DIFF_QK_DIM, qt, zero)
        dst[1] = jnp.where(row >= DIFF_QK_DIM, qt, zero)

    def reset():
        m_scr[...] = jnp.full(m_scr.shape, NEG, F32)
        acc_scr[...] = jnp.zeros(acc_scr.shape, F32)

    def scores(w, j, slot):
        start = pl.multiple_of(j * tk, tk)
        kb = k_ref[pl.ds(start, tk), :]
        for c in range(2):
            s = jnp.dot(kb, w[c], preferred_element_type=F32)
            s_scr[slot, c] = s
            bm_scr[slot, c] = jnp.max(s, axis=0, keepdims=True)

    def accumulate(j, slot):
        start = pl.multiple_of(j * tk, tk)
        vb = vt_ref[:, pl.ds(start, tk)]
        for c in range(2):
            m_old = m_scr[c]
            m_new = jnp.maximum(m_old, bm_scr[slot, c])
            alpha = jnp.exp2(m_old - m_new)
            p = jnp.exp2(s_scr[slot, c] - m_new).astype(BF16)
            acc_scr[c] = alpha * acc_scr[c] + jnp.dot(vb, p, preferred_element_type=F32)
            m_scr[c] = m_new

    def steps(j, to_next_query_block):
        for u in range(unroll):
            if to_next_query_block and u == unroll - 1:
                scores(wn_scr, 0, 0)
            else:
                scores(w_scr, j + u + 1, (u + 1) % 2)
            accumulate(j + u, u % 2)

    def output(qi):
        rows = pl.ds(pl.multiple_of(qi * tq, tq), tq)
        lam_p = lam_ref[...]
        lam = (jnp.exp(jnp.sum(lam_p[0:1] * lam_p[1:2], axis=-1, keepdims=True))
               - jnp.exp(jnp.sum(lam_p[2:3] * lam_p[3:4], axis=-1, keepdims=True)) + lam_init)
        a0, a1 = fin_scr[0], fin_scr[1]
        o_t = (a0[:DIFF_V_DIM] * (1.0 / a0[DIFF_V_DIM:DIFF_V_DIM + 1])
               - lam * (a1[:DIFF_V_DIM] * (1.0 / a1[DIFF_V_DIM:DIFF_V_DIM + 1])))
        ms = jnp.mean(o_t * o_t, axis=0, keepdims=True)
        o = (o_t * lax.rsqrt(ms + RMS_EPS)).T
        g = gate_ref[rows, :]
        silu = g / (1.0 + jnp.exp(-g))
        o_ref[rows, :] = (o * (subln_ref[...] * (1.0 - lam_init)) * silu).astype(o_ref.dtype)

    load_w(w_scr, 0)
    reset()
    fin_scr[...] = jnp.ones(fin_scr.shape, F32)
    scores(w_scr, 0, 0)

    def query_block(qi, carry):
        output(jnp.maximum(qi - 1, 0))
        steps(0, False)
        if n_body > 2:
            def body(i, c):
                steps(unroll * i, False)
                return c
            lax.fori_loop(1, n_body - 1, body, 0)
        load_w(wn_scr, jnp.minimum(qi + 1, n_q - 1))
        steps(nk - unroll, True)
        fin_scr[...] = acc_scr[...]
        reset()
        w_scr[...] = wn_scr[...]
        return carry

    lax.fori_loop(0, n_q, query_block, 0)
    output(n_q - 1)


def _diff_attention(lam_params, subln, qt, k, vt, gate, lam_init):
    batch, seq, _ = k.shape
    tq = min(ATTN_BLOCK, seq)
    tk = min(ATTN_BLOCK, seq // 2)
    unroll = next(u for u in (8, 4, 2) if seq // tk >= 2 * u)
    assert seq % (unroll * tk) == 0 and seq // (unroll * tk) >= 2
    tqg = min(DIFF_Q_GROUP * tq, seq)
    return pl.pallas_call(
        functools.partial(_diff_attn_kernel, tq=tq, tk=tk, unroll=unroll, lam_init=lam_init),
        grid=(batch, DIFF_HEADS, seq // tqg),
        in_specs=[
            pl.BlockSpec(lam_params.shape, lambda b, h, i: (0, 0)),
            pl.BlockSpec(subln.shape, lambda b, h, i: (0, 0)),
            pl.BlockSpec((None, LANES, tqg), lambda b, h, i: (b, h, i)),
            pl.BlockSpec((None, seq, LANES), lambda b, h, i: (b, 0, h)),
            pl.BlockSpec((None, VT_ROWS, seq), lambda b, h, i: (b, h, 0)),
            pl.BlockSpec((None, tqg, LANES), lambda b, h, i: (b, i, h)),
        ],
        out_specs=pl.BlockSpec((None, tqg, LANES), lambda b, h, i: (b, i, h)),
        out_shape=jax.ShapeDtypeStruct((batch, seq, DIFF_WIDTH), BF16),
        scratch_shapes=[
            pltpu.VMEM((2, LANES, tq), BF16),
            pltpu.VMEM((2, LANES, tq), BF16),
            pltpu.VMEM((2, 2, tk, tq), F32),
            pltpu.VMEM((2, 2, 1, tq), F32),
            pltpu.VMEM((2, 1, tq), F32),
            pltpu.VMEM((2, VT_ROWS, tq), F32),
            pltpu.VMEM((2, VT_ROWS, tq), F32),
        ],
        compiler_params=_cparams(("parallel", "parallel", "parallel")),
        name="diff_attn",
    )(lam_params, subln, qt, k, vt, gate)


def _band_window(prev_ref, cur_ref, next_ref, unit, n_units, lo):
    r, q = DIL_RADIUS, BAND_UNIT
    lanes = slice(lo, lo + LANES)
    pieces = []
    if unit == 0:
        pieces.append(prev_ref[:, lanes])
    first = max(unit * q - r, 0)
    last = min(unit * q + q + r, n_units * q)
    pieces.append(cur_ref[first:last, lanes])
    if unit == n_units - 1:
        pieces.append(next_ref[:, lanes])
    return pieces[0] if len(pieces) == 1 else jnp.concatenate(pieces, axis=0)


def _band_attn_kernel(q_ref, kp_ref, kc_ref, kn_ref, vp_ref, vc_ref, vn_ref, o_ref, lse_ref, *, sub_len):
    t_rows = q_ref.shape[0]
    n_units = t_rows // BAND_UNIT
    period = min(sub_len, t_rows)
    tw = BAND_UNIT + 2 * DIL_RADIUS
    i = pl.program_id(0)
    row = lax.broadcasted_iota(jnp.int32, (BAND_UNIT, tw), 0)
    col = lax.broadcasted_iota(jnp.int32, (BAND_UNIT, tw), 1)
    rel = col - row
    band = jnp.where(rel >= 0, 0.0, NEG)
    band = jnp.where(rel <= 2 * DIL_RADIUS, band, NEG).astype(F32)
    for u in range(n_units):
        bias = band
        offset = (u * BAND_UNIT) % period
        may_start, may_end = offset == 0, offset == period - BAND_UNIT
        if may_start or may_end:
            kpos = lax.rem(i * t_rows + u * BAND_UNIT, sub_len) - DIL_RADIUS + col
            if may_start:
                bias = jnp.where(kpos >= 0, bias, NEG)
            if may_end:
                bias = jnp.where(kpos < sub_len, bias, NEG)
        rows = slice(u * BAND_UNIT, (u + 1) * BAND_UNIT)
        for h in range(DIL_HEADS):
            lo = h * LANES
            kwin = _band_window(kp_ref, kc_ref, kn_ref, u, n_units, lo)
            vwin = _band_window(vp_ref, vc_ref, vn_ref, u, n_units, lo)
            s = lax.dot_general(q_ref[rows, lo:lo + LANES], kwin,
                                (((1,), (1,)), ((), ())), preferred_element_type=F32)
            s = s + bias
            m = jnp.max(s, axis=-1, keepdims=True)
            p = jnp.exp2(s - m)
            den = jnp.sum(p, axis=-1, keepdims=True)
            o = jnp.dot(p.astype(BF16), vwin, preferred_element_type=F32)
            o_ref[rows, lo:lo + LANES] = (o * (1.0 / den)).astype(o_ref.dtype)
            lse = (m + jnp.log2(den)) * LN2
            lse_ref[rows, h * LSE_LANES:(h + 1) * LSE_LANES] = jnp.broadcast_to(lse, (BAND_UNIT, LSE_LANES))


def _band_attention(q, k, v, sub_len):
    n_rows = q.shape[0]
    t_rows = min(BAND_ROWS, n_rows)
    assert n_rows % t_rows == 0 and sub_len % BAND_UNIT == 0
    assert t_rows % sub_len == 0 or sub_len % t_rows == 0
    per = t_rows // DIL_RADIUS
    nhalo = n_rows // DIL_RADIUS
    cur = pl.BlockSpec((t_rows, DIL_WIDTH), lambda i: (i, 0))
    prev = pl.BlockSpec((DIL_RADIUS, DIL_WIDTH), lambda i: (jnp.maximum(i * per - 1, 0), 0))
    nxt = pl.BlockSpec((DIL_RADIUS, DIL_WIDTH), lambda i: (jnp.minimum((i + 1) * per, nhalo - 1), 0))
    return pl.pallas_call(
        functools.partial(_band_attn_kernel, sub_len=sub_len),
        grid=(n_rows // t_rows,),
        in_specs=[cur, prev, cur, nxt, prev, cur, nxt],
        out_specs=[cur, pl.BlockSpec((t_rows, LANES), lambda i: (i, 0))],
        out_shape=[jax.ShapeDtypeStruct((n_rows, DIL_WIDTH), BF16), jax.ShapeDtypeStruct((n_rows, LANES), F32)],
        compiler_params=_cparams(("parallel",)),
        name=f"band_attn{sub_len}",
    )(q, k, k, k, v, v, v)


def _rms(x, g):
    ms = jnp.mean(x * x, axis=-1, keepdims=True)
    return x * lax.rsqrt(ms + RMS_EPS) * g


def _row_parts(tm):
    part = tm // 2 if tm >= 2 * MXU_DIM else tm
    return [slice(r0, r0 + part) for r0 in range(0, tm, part)]


def _tail_kernel(x_ref, ya_ref, o0_ref, l0_ref, o1_ref, l1_ref, o2_ref, l2_ref, bg_ref, p_ref,
                 wout_ref, wgate_ref, wproj_ref, plen_ref, fnorm_ref, y_ref,
                 o1_scr, l1_scr, o2_scr, l2_scr, yb_scr):
    tm = x_ref.shape[0]
    for src, dst in ((o1_ref, o1_scr), (o2_ref, o2_scr)):
        d = src.shape[0]
        for r in range(d):
            for h in range(DIL_HEADS):
                dst[h, pl.ds(r, tm // d, stride=d), :] = src[r, :, h * LANES:(h + 1) * LANES].astype(F32)
    for src, dst in ((l1_ref, l1_scr), (l2_ref, l2_scr)):
        d = src.shape[0]
        for r in range(d):
            dst[pl.ds(r, tm // d, stride=d), :] = src[r]
    for rows in _row_parts(tm):
        n = rows.stop - rows.start
        l0, l1, l2 = l0_ref[rows, :], l1_scr[rows, :], l2_scr[rows, :]
        mx = jnp.maximum(jnp.maximum(l0, l1), l2)
        e0, e1, e2 = jnp.exp(l0 - mx), jnp.exp(l1 - mx), jnp.exp(l2 - mx)
        inv = 1.0 / (e0 + e1 + e2)
        w0, w1, w2 = e0 * inv, e1 * inv, e2 * inv
        for h in range(DIL_HEADS):
            slab = slice(h * LANES, (h + 1) * LANES)
            wide = lambda w: jnp.broadcast_to(w[:, h * LSE_LANES:h * LSE_LANES + 1], (n, LANES))
            ob = wide(w0) * o0_ref[rows, slab] + wide(w1) * o1_scr[h, rows, :] + wide(w2) * o2_scr[h, rows, :]
            bg = bg_ref[rows, slab]
            yb_scr[rows, slab] = (ob * (bg / (1.0 + jnp.exp(-bg)))).astype(BF16)
        hid = (x_ref[rows, :]
               + jnp.dot(ya_ref[rows, :], wout_ref[:DIFF_WIDTH, :], preferred_element_type=F32)
               + jnp.dot(yb_scr[rows, :], wout_ref[DIFF_WIDTH:, :], preferred_element_type=F32))
        nrm = _rms(hid, plen_ref[...]).astype(BF16)
        gate = 1.0 / (1.0 + jnp.exp(-jnp.dot(nrm, wgate_ref[...], preferred_element_type=F32)))
        hid = hid + gate * jnp.dot(p_ref[rows, :].astype(BF16), wproj_ref[...], preferred_element_type=F32)
        y_ref[rows, :] = _rms(hid, fnorm_ref[...])


def _tail(x, ya, o0, l0, o1, l1, o2, l2, bg, p, wout, wgate, wproj, plen, fnorm):
    batch, seq, _ = x.shape
    tm = min(ROW_TILE, seq)
    row = lambda b, i: (b, i, 0)
    full = lambda a: pl.BlockSpec(a.shape, lambda b, i: (0,) * a.ndim)
    wide = pl.BlockSpec((None, tm, D_MODEL), row)
    half = pl.BlockSpec((None, tm, DIL_WIDTH), row)
    lse = pl.BlockSpec((None, tm, LANES), row)

    def res(a):
        d, width = a.shape[1], a.shape[3]
        return pl.BlockSpec((None, d, tm // d, width), lambda b, i: (b, 0, i, 0))

    return pl.pallas_call(
        _tail_kernel,
        grid=(batch, seq // tm),
        in_specs=[wide, half, half, lse, res(o1), res(l1), res(o2), res(l2), half,
                  pl.BlockSpec((None, tm, PLE_DIM), row),
                  full(wout), full(wgate), full(wproj), full(plen), full(fnorm)],
        out_specs=wide,
        out_shape=jax.ShapeDtypeStruct((batch, seq, D_MODEL), F32),
        scratch_shapes=[pltpu.VMEM((DIL_HEADS, tm, LANES), F32), pltpu.VMEM((tm, LANES), F32),
                        pltpu.VMEM((DIL_HEADS, tm, LANES), F32), pltpu.VMEM((tm, LANES), F32),
                        pltpu.VMEM((tm, DIL_WIDTH), BF16)],
        compiler_params=_cparams(("parallel", "parallel")),
        name="tail",
    )(x, ya, o0, l0, o1, l1, o2, l2, bg, p, wout, wgate, wproj, plen, fnorm)


def _layer(x, p, layer_idx, tabs, norm_mix, w_in, lam_params, subln, wout, wgate, wproj, plen, fnorm):
    batch, seq, _ = x.shape
    tabs_a, tabs_b = tabs
    lam_init = 0.8 - 0.6 * math.exp(-0.3 * layer_idx)

    qt, k, vt, ag = _proj_diff(x, norm_mix, w_in, tabs_a)
    ya = _diff_attention(lam_params, subln, qt, k, vt, ag, lam_init)

    outs, lses, bg = [], [], None
    for g, (_, dilation) in enumerate(DIL_PATTERNS):
        res = _proj_dil(x, norm_mix, w_in, tabs_b, g, dilation, with_gate=(g == 0))
        if g == 0:
            bg = res[3]
        sub = seq // dilation
        q, kk, v = (a.reshape(batch * seq, DIL_WIDTH) for a in res[:3])
        o, lse = _band_attention(q, kk, v, sub)
        outs.append(o.reshape(batch, dilation, sub, DIL_WIDTH))
        lses.append(lse.reshape(batch, dilation, sub, LANES))

    return _tail(x, ya, outs[0].reshape(batch, seq, DIL_WIDTH), lses[0].reshape(batch, seq, LANES),
                 outs[1], lses[1], outs[2], lses[2], bg, p, wout, wgate, wproj, plen, fnorm)


def kernel(x_prompt, x_sample, p_prompt, p_sample, norm_mix, w_in, lam_q1, lam_k1, lam_q2, lam_k2,
           subln, w_out, ple_norm, w_ple_gate, w_ple_proj, final_norm):
    depth = w_in.shape[0]
    assert depth == 1, "the fused tail applies the final norm, so it handles a single layer"
    i = 0
    seqs = (x_prompt.shape[1], x_sample.shape[1])
    tm = _proj_tile(min(seqs))
    assert all(_proj_tile(n) == tm for n in seqs)
    tabs = _rope_tables(tm, max(seqs) // tm)
    lam_params = jnp.stack([lam_q1[i], lam_k1[i], lam_q2[i], lam_k2[i]], axis=0)
    args = (tabs, norm_mix[i][None, :], w_in[i].astype(BF16), lam_params, subln[i][None, :],
            w_out[i].astype(BF16), w_ple_gate[i].astype(BF16), w_ple_proj[i].astype(BF16),
            ple_norm[i][None, :], final_norm[None, :])
    y_prompt = _layer(x_prompt, p_prompt[i], i, *args)
    y_sample = _layer(x_sample, p_sample[i], i, *args)
    return (y_prompt, y_sample)
```

```python
import functools
import math

import jax
import jax.numpy as jnp
import numpy as np
from jax import lax
from jax.experimental import pallas as pl
from jax.experimental.pallas import tpu as pltpu

F32 = jnp.float32
BF16 = jnp.bfloat16

D_MODEL = 2048
PLE_DIM = 256
ROPE_THETA = 10000.0
RMS_EPS = 1e-6
NEG = -1e30
LOG2E = math.log2(math.e)
LN2 = math.log(2.0)

DIFF_HEADS = 4
DIFF_QK_DIM = 64
DIFF_V_DIM = 128
DIFF_WIDTH = DIFF_HEADS * DIFF_V_DIM
DIL_PATTERNS = ((128, 1), (512, 4), (2048, 16))
DIL_HEADS = 4
DIL_HEAD_DIM = 128
DIL_WIDTH = DIL_HEADS * DIL_HEAD_DIM
DIL_RADIUS = 64
PERM_MIN_DILATION = 16
BAND_UNIT = 128
BAND_ROWS = 4096
BAND_MIN_STEPS = 4
ROW_TILE = 512
ATTN_BLOCK = 512
assert all(w // (2 * d) == DIL_RADIUS for w, d in DIL_PATTERNS)

LANES = 128
MXU_DIM = 256
LSE_LANES = LANES // DIL_HEADS
BF16_SUBLANES = 16
DIFF_Q_GROUP = 16
VT_ROWS = DIFF_V_DIM + BF16_SUBLANES
VMEM_LIMIT_BYTES = 56 * 1024 * 1024


def _cparams(semantics):
    return pltpu.CompilerParams(dimension_semantics=semantics, vmem_limit_bytes=VMEM_LIMIT_BYTES)


def _rope_tables(tm, n_blocks):
    def layout(dim, reps):
        inv = np.float32(ROPE_THETA) ** (-np.arange(0, dim, 2, dtype=np.float32) / np.float32(dim))
        freq = np.tile(inv.astype(np.float64), reps)[None, :]
        base = (np.arange(n_blocks, dtype=np.float64) * tm)[:, None] * freq
        row = np.arange(tm, dtype=np.float64)[:, None] * freq
        blk = [jnp.asarray(f(base)[:, None, :], F32) for f in (np.cos, np.sin)]
        return blk + [jnp.asarray(f(row), F32) for f in (np.cos, np.sin)]

    return layout(DIFF_QK_DIM, LANES // (DIFF_QK_DIM // 2)), layout(DIL_HEAD_DIM, LANES // (DIL_HEAD_DIM // 2))


def _angle_add(cb_ref, sb_ref, cr_ref, sr_ref):
    cb, sb, cr, sr = cb_ref[...], sb_ref[...], cr_ref[...], sr_ref[...]
    return cb * cr - sb * sr, sb * cr + cb * sr


def _rope64(x, cos, sin_up, sin_dn):
    return x * cos + pltpu.roll(x, 96, 1) * sin_up + pltpu.roll(x, 32, 1) * sin_dn


def _rope128(x, cos, sin):
    return x * cos + pltpu.roll(x, 64, 1) * sin


def _normalize(x_ref, g_ref):
    x = x_ref[...]
    ms = jnp.mean(x * x, axis=-1, keepdims=True)
    return (x * lax.rsqrt(ms + RMS_EPS) * g_ref[...]).astype(BF16)


def _proj_diff_kernel(x_ref, g_ref, w_ref, cb_ref, sb_ref, cr_ref, sr_ref, qt_ref, k_ref, vt_ref, gate_ref):
    z = jnp.dot(_normalize(x_ref, g_ref), w_ref[...], preferred_element_type=F32)
    cos, sin = _angle_add(cb_ref, sb_ref, cr_ref, sr_ref)
    half = lax.broadcasted_iota(jnp.int32, (1, LANES), 1) % DIFF_QK_DIM < DIFF_QK_DIM // 2
    sup = sin * jnp.where(half, -1.0, 0.0)
    sdn = sin * jnp.where(half, 0.0, 1.0)
    qscale = DIFF_QK_DIM**-0.5 * LOG2E
    for h in range(DIFF_HEADS):
        lo = h * LANES
        q = _rope64(z[:, lo:lo + LANES], cos, sup, sdn) * qscale
        qt_ref[lo:lo + LANES, :] = q.T.astype(BF16)
        k = _rope64(z[:, DIFF_WIDTH + lo:DIFF_WIDTH + lo + LANES], cos, sup, sdn)
        k_ref[:, lo:lo + LANES] = k.astype(BF16)
        v = z[:, 2 * DIFF_WIDTH + lo:2 * DIFF_WIDTH + lo + LANES]
        vlo = h * VT_ROWS
        vt_ref[vlo:vlo + LANES, :] = v.T.astype(BF16)
        vt_ref[vlo + LANES:vlo + VT_ROWS, :] = jnp.ones((VT_ROWS - LANES, v.shape[0]), BF16)
    gate_ref[...] = z[:, 3 * DIFF_WIDTH:]


def _proj_dil_kernel(x_ref, g_ref, *refs, dilation, with_gate, use_perm):
    n_w = 4 if with_gate else 3
    w_refs, tab_refs, refs = refs[:n_w], refs[n_w:n_w + 4], refs[n_w + 4:]
    perm_ref = None
    if use_perm:
        perm_ref, refs = refs[0], refs[1:]
    q_ref, k_ref, v_ref = refs[:3]
    u = _normalize(x_ref, g_ref)
    zq, zk, zv = (jnp.dot(u, w[...], preferred_element_type=F32) for w in w_refs[:3])
    cos, sin = _angle_add(*tab_refs)
    first = lax.broadcasted_iota(jnp.int32, (1, LANES), 1) < DIL_HEAD_DIM // 2
    sin = sin * jnp.where(first, -1.0, 1.0)
    qscale = DIL_HEAD_DIM**-0.5 * LOG2E
    tm = u.shape[0]
    rows = tm // dilation
    if dilation == 1:
        for h in range(DIL_HEADS):
            lanes = slice(h * LANES, (h + 1) * LANES)
            q_ref[0, :, lanes] = (_rope128(zq[:, lanes], cos, sin) * qscale).astype(BF16)
            k_ref[0, :, lanes] = _rope128(zk[:, lanes], cos, sin).astype(BF16)
        v_ref[0] = zv.astype(BF16)
    elif perm_ref is not None:
        zb = jnp.concatenate(
            [jnp.concatenate([(_rope128(zq[:, h * LANES:(h + 1) * LANES], cos, sin) * qscale).astype(BF16)
                              for h in range(DIL_HEADS)], axis=1),
             jnp.concatenate([_rope128(zk[:, h * LANES:(h + 1) * LANES], cos, sin).astype(BF16)
                              for h in range(DIL_HEADS)], axis=1),
             zv.astype(BF16)], axis=1)
        part = perm_ref.shape[0]
        sub = part // dilation
        for p0 in range(0, tm, part):
            zp = jnp.dot(perm_ref[...], zb[p0:p0 + part], preferred_element_type=F32).astype(BF16)
            for r in range(dilation):
                for j, out_ref in enumerate((q_ref, k_ref, v_ref)):
                    out_ref[r, p0 // dilation:p0 // dilation + sub, :] = (
                        zp[r * sub:(r + 1) * sub, j * DIL_WIDTH:(j + 1) * DIL_WIDTH])
    else:
        stage_ref = refs[-1]
        for h in range(DIL_HEADS):
            lanes = slice(h * LANES, (h + 1) * LANES)
            stage_ref[h] = _rope128(zq[:, lanes], cos, sin) * qscale
            stage_ref[DIL_HEADS + h] = _rope128(zk[:, lanes], cos, sin)
            stage_ref[2 * DIL_HEADS + h] = zv[:, lanes]
        for r in range(dilation):
            for j, out_ref in enumerate((q_ref, k_ref, v_ref)):
                for h in range(DIL_HEADS):
                    blk = stage_ref[j * DIL_HEADS + h, pl.ds(r, rows, stride=dilation), :]
                    out_ref[r, :, h * LANES:(h + 1) * LANES] = blk.astype(BF16)
    if with_gate:
        refs[3][...] = jnp.dot(u, w_refs[3][...], preferred_element_type=F32)


def _proj_tile(seq):
    return min(ROW_TILE, seq)


def _rope_specs(tm):
    blk = pl.BlockSpec((None, 1, LANES), lambda b, i: (i, 0, 0))
    row = pl.BlockSpec((tm, LANES), lambda b, i: (0, 0))
    return [blk, blk, row, row]


def _proj_diff(x, g, w, tabs):
    batch, seq, _ = x.shape
    tm = _proj_tile(seq)
    row = lambda b, i: (b, i, 0)
    col = lambda b, i: (b, 0, i)
    return pl.pallas_call(
        _proj_diff_kernel,
        grid=(batch, seq // tm),
        in_specs=[
            pl.BlockSpec((None, tm, D_MODEL), row),
            pl.BlockSpec((1, D_MODEL), lambda b, i: (0, 0)),
            pl.BlockSpec((D_MODEL, 4 * DIFF_WIDTH), lambda b, i: (0, 0)),
            *_rope_specs(tm),
        ],
        out_specs=[
            pl.BlockSpec((None, DIFF_WIDTH, tm), col),
            pl.BlockSpec((None, tm, DIFF_WIDTH), row),
            pl.BlockSpec((None, DIFF_HEADS * VT_ROWS, tm), col),
            pl.BlockSpec((None, tm, DIFF_WIDTH), row),
        ],
        out_shape=[
            jax.ShapeDtypeStruct((batch, DIFF_WIDTH, seq), BF16),
            jax.ShapeDtypeStruct((batch, seq, DIFF_WIDTH), BF16),
            jax.ShapeDtypeStruct((batch, DIFF_HEADS * VT_ROWS, seq), BF16),
            jax.ShapeDtypeStruct((batch, seq, DIFF_WIDTH), F32),
        ],
        compiler_params=_cparams(("parallel", "parallel")),
        name="proj_diff",
    )(x, g, w, *tabs)


def _proj_dil(x, g, w, tabs, group, dilation, with_gate):
    batch, seq, _ = x.shape
    tm = _proj_tile(seq)
    rows = tm // dilation
    sub = seq // dilation
    row = lambda b, i: (b, i, 0)
    res = lambda b, i: (b, 0, i, 0)
    res_spec = pl.BlockSpec((None, dilation, rows, DIL_WIDTH), res)
    res_shape = jax.ShapeDtypeStruct((batch, dilation, sub, DIL_WIDTH), BF16)
    out_specs = [res_spec, res_spec, res_spec]
    out_shape = [res_shape, res_shape, res_shape]
    if with_gate:
        out_specs.append(pl.BlockSpec((None, tm, DIL_WIDTH), row))
        out_shape.append(jax.ShapeDtypeStruct((batch, seq, DIL_WIDTH), F32))
    use_perm = dilation >= PERM_MIN_DILATION
    scratch = [] if dilation == 1 or use_perm else [pltpu.VMEM((3 * DIL_HEADS, tm, LANES), F32)]
    perm_args, perm_specs = [], []
    if use_perm:
        part = min(MXU_DIM, tm)
        sub = part // dilation
        src = (np.arange(part) % sub) * dilation + np.arange(part) // sub
        perm_args = [jnp.asarray(np.eye(part, dtype=np.float32)[src], BF16)]
        perm_specs = [pl.BlockSpec((part, part), lambda b, i: (0, 0))]
    groups = len(DIL_PATTERNS)
    col_blocks = [4 + j * groups + group for j in range(3)] + ([4 + 3 * groups] if with_gate else [])
    w_specs = [pl.BlockSpec((D_MODEL, DIL_WIDTH), functools.partial(lambda b, i, c: (0, c), c=c))
               for c in col_blocks]
    return pl.pallas_call(
        functools.partial(_proj_dil_kernel, dilation=dilation, with_gate=with_gate, use_perm=use_perm),
        grid=(batch, seq // tm),
        in_specs=[
            pl.BlockSpec((None, tm, D_MODEL), row),
            pl.BlockSpec((1, D_MODEL), lambda b, i: (0, 0)),
            *w_specs,
            *_rope_specs(tm),
            *perm_specs,
        ],
        out_specs=out_specs,
        out_shape=out_shape,
        scratch_shapes=scratch,
        compiler_params=_cparams(("parallel", "parallel")),
        name=f"proj_dil{dilation}",
    )(x, g, *([w] * len(w_specs)), *tabs, *perm_args)


def _diff_attn_kernel(lam_ref, subln_ref, qt_ref, k_ref, vt_ref, gate_ref, o_ref,
                      w_scr, wn_scr, s_scr, bm_scr, m_scr, acc_scr, fin_scr, *, tq, tk, unroll, lam_init):
    nk = k_ref.shape[0] // tk
    n_q = qt_ref.shape[1] // tq
    n_body = nk // unroll

    def load_w(dst, qi):
        qt = qt_ref[:, pl.ds(pl.multiple_of(qi * tq, tq), tq)]
        row = lax.broadcasted_iota(jnp.int32, qt.shape, 0)
        zero = jnp.zeros_like(qt)
        dst[0] = jnp.where(row < DIFF_QK_DIM, qt, zero)
        dst[1] = jnp.where(row >= DIFF_QK_DIM, qt, zero)

    def reset():
        m_scr[...] = jnp.full(m_scr.shape, NEG, F32)
        acc_scr[...] = jnp.zeros(acc_scr.shape, F32)

    def scores(w, j, slot):
        start = pl.multiple_of(j * tk, tk)
        kb = k_ref[pl.ds(start, tk), :]
        for c in range(2):
            s = jnp.dot(kb, w[c], preferred_element_type=F32)
            s_scr[slot, c] = s
            bm_scr[slot, c] = jnp.max(s, axis=0, keepdims=True)

    def accumulate(j, slot):
        start = pl.multiple_of(j * tk, tk)
        vb = vt_ref[:, pl.ds(start, tk)]
        for c in range(2):
            m_old = m_scr[c]
            m_new = jnp.maximum(m_old, bm_scr[slot, c])
            alpha = jnp.exp2(m_old - m_new)
            p = jnp.exp2(s_scr[slot, c] - m_new).astype(BF16)
            acc_scr[c] = alpha * acc_scr[c] + jnp.dot(vb, p, preferred_element_type=F32)
            m_scr[c] = m_new

    def steps(j, to_next_query_block):
        for u in range(unroll):
            if to_next_query_block and u == unroll - 1:
                scores(wn_scr, 0, 0)
            else:
                scores(w_scr, j + u + 1, (u + 1) % 2)
            accumulate(j + u, u % 2)

    def output(qi):
        rows = pl.ds(pl.multiple_of(qi * tq, tq), tq)
        lam_p = lam_ref[...]
        lam = (jnp.exp(jnp.sum(lam_p[0:1] * lam_p[1:2], axis=-1, keepdims=True))
               - jnp.exp(jnp.sum(lam_p[2:3] * lam_p[3:4], axis=-1, keepdims=True)) + lam_init)
        a0, a1 = fin_scr[0], fin_scr[1]
        o_t = (a0[:DIFF_V_DIM] * (1.0 / a0[DIFF_V_DIM:DIFF_V_DIM + 1])
               - lam * (a1[:DIFF_V_DIM] * (1.0 / a1[DIFF_V_DIM:DIFF_V_DIM + 1])))
        ms = jnp.mean(o_t * o_t, axis=0, keepdims=True)
        o = (o_t * lax.rsqrt(ms + RMS_EPS)).T
        g = gate_ref[rows, :]
        silu = g / (1.0 + jnp.exp(-g))
        o_ref[rows, :] = (o * (subln_ref[...] * (1.0 - lam_init)) * silu).astype(o_ref.dtype)

    load_w(w_scr, 0)
    reset()
    fin_scr[...] = jnp.ones(fin_scr.shape, F32)
    scores(w_scr, 0, 0)

    def query_block(qi, carry):
        output(jnp.maximum(qi - 1, 0))
        steps(0, False)
        if n_body > 2:
            def body(i, c):
                steps(unroll * i, False)
                return c
            lax.fori_loop(1, n_body - 1, body, 0)
        load_w(wn_scr, jnp.minimum(qi + 1, n_q - 1))
        steps(nk - unroll, True)
        fin_scr[...] = acc_scr[...]
        reset()
        w_scr[...] = wn_scr[...]
        return carry

    lax.fori_loop(0, n_q, query_block, 0)
    output(n_q - 1)


def _diff_attention(lam_params, subln, qt, k, vt, gate, lam_init):
    batch, seq, _ = k.shape
    tq = min(ATTN_BLOCK, seq)
    tk = min(ATTN_BLOCK, seq // 2)
    unroll = next(u for u in (8, 4, 2) if seq // tk >= 2 * u)
    assert seq % (unroll * tk) == 0 and seq // (unroll * tk) >= 2
    tqg = min(DIFF_Q_GROUP * tq, seq)
    return pl.pallas_call(
        functools.partial(_diff_attn_kernel, tq=tq, tk=tk, unroll=unroll, lam_init=lam_init),
        grid=(batch, DIFF_HEADS, seq // tqg),
        in_specs=[
            pl.BlockSpec(lam_params.shape, lambda b, h, i: (0, 0)),
            pl.BlockSpec(subln.shape, lambda b, h, i: (0, 0)),
            pl.BlockSpec((None, LANES, tqg), lambda b, h, i: (b, h, i)),
            pl.BlockSpec((None, seq, LANES), lambda b, h, i: (b, 0, h)),
            pl.BlockSpec((None, VT_ROWS, seq), lambda b, h, i: (b, h, 0)),
            pl.BlockSpec((None, tqg, LANES), lambda b, h, i: (b, i, h)),
        ],
        out_specs=pl.BlockSpec((None, tqg, LANES), lambda b, h, i: (b, i, h)),
        out_shape=jax.ShapeDtypeStruct((batch, seq, DIFF_WIDTH), BF16),
        scratch_shapes=[
            pltpu.VMEM((2, LANES, tq), BF16),
            pltpu.VMEM((2, LANES, tq), BF16),
            pltpu.VMEM((2, 2, tk, tq), F32),
            pltpu.VMEM((2, 2, 1, tq), F32),
            pltpu.VMEM((2, 1, tq), F32),
            pltpu.VMEM((2, VT_ROWS, tq), F32),
            pltpu.VMEM((2, VT_ROWS, tq), F32),
        ],
        compiler_params=_cparams(("parallel", "parallel", "parallel")),
        name="diff_attn",
    )(lam_params, subln, qt, k, vt, gate)


def _band_window(prev_ref, cur_ref, next_ref, unit, n_units, lo):
    r, q = DIL_RADIUS, BAND_UNIT
    lanes = slice(lo, lo + LANES)
    pieces = []
    if unit == 0:
        pieces.append(prev_ref[:, lanes])
    first = max(unit * q - r, 0)
    last = min(unit * q + q + r, n_units * q)
    pieces.append(cur_ref[first:last, lanes])
    if unit == n_units - 1:
        pieces.append(next_ref[:, lanes])
    return pieces[0] if len(pieces) == 1 else jnp.concatenate(pieces, axis=0)


def _band_attn_kernel(q_ref, kp_ref, kc_ref, kn_ref, vp_ref, vc_ref, vn_ref, o_ref, lse_ref, *, sub_len):
    t_rows = q_ref.shape[0]
    n_units = t_rows // BAND_UNIT
    period = min(sub_len, t_rows)
    tw = BAND_UNIT + 2 * DIL_RADIUS
    i = pl.program_id(0)
    row = lax.broadcasted_iota(jnp.int32, (BAND_UNIT, tw), 0)
    col = lax.broadcasted_iota(jnp.int32, (BAND_UNIT, tw), 1)
    rel = col - row
    band = jnp.where(rel >= 0, 0.0, NEG)
    band = jnp.where(rel <= 2 * DIL_RADIUS, band, NEG).astype(F32)
    for u in range(n_units):
        bias = band
        offset = (u * BAND_UNIT) % period
        may_start, may_end = offset == 0, offset == period - BAND_UNIT
        if may_start or may_end:
            kpos = lax.rem(i * t_rows + u * BAND_UNIT, sub_len) - DIL_RADIUS + col
            if may_start:
                bias = jnp.where(kpos >= 0, bias, NEG)
            if may_end:
                bias = jnp.where(kpos < sub_len, bias, NEG)
        rows = slice(u * BAND_UNIT, (u + 1) * BAND_UNIT)
        for h in range(DIL_HEADS):
            lo = h * LANES
            kwin = _band_window(kp_ref, kc_ref, kn_ref, u, n_units, lo)
            vwin = _band_window(vp_ref, vc_ref, vn_ref, u, n_units, lo)
            s = lax.dot_general(q_ref[rows, lo:lo + LANES], kwin,
                                (((1,), (1,)), ((), ())), preferred_element_type=F32)
            s = s + bias
            m = jnp.max(s, axis=-1, keepdims=True)
            p = jnp.exp2(s - m)
            den = jnp.sum(p, axis=-1, keepdims=True)
            o = jnp.dot(p.astype(BF16), vwin, preferred_element_type=F32)
            o_ref[rows, lo:lo + LANES] = (o * (1.0 / den)).astype(o_ref.dtype)
            lse = (m + jnp.log2(den)) * LN2
            lse_ref[rows, h * LSE_LANES:(h + 1) * LSE_LANES] = jnp.broadcast_to(lse, (BAND_UNIT, LSE_LANES))


def _band_attention(q, k, v, sub_len):
    n_rows = q.shape[0]
    t_rows = min(BAND_ROWS, n_rows // BAND_MIN_STEPS)
    assert n_rows % t_rows == 0 and sub_len % BAND_UNIT == 0
    assert t_rows % sub_len == 0 or sub_len % t_rows == 0
    per = t_rows // DIL_RADIUS
    nhalo = n_rows // DIL_RADIUS
    cur = pl.BlockSpec((t_rows, DIL_WIDTH), lambda i: (i, 0))
    prev = pl.BlockSpec((DIL_RADIUS, DIL_WIDTH), lambda i: (jnp.maximum(i * per - 1, 0), 0))
    nxt = pl.BlockSpec((DIL_RADIUS, DIL_WIDTH), lambda i: (jnp.minimum((i + 1) * per, nhalo - 1), 0))
    return pl.pallas_call(
        functools.partial(_band_attn_kernel, sub_len=sub_len),
        grid=(n_rows // t_rows,),
        in_specs=[cur, prev, cur, nxt, prev, cur, nxt],
        out_specs=[cur, pl.BlockSpec((t_rows, LANES), lambda i: (i, 0))],
        out_shape=[jax.ShapeDtypeStruct((n_rows, DIL_WIDTH), BF16), jax.ShapeDtypeStruct((n_rows, LANES), F32)],
        compiler_params=_cparams(("parallel",)),
        name=f"band_attn{sub_len}",
    )(q, k, k, k, v, v, v)


def _rms(x, g):
    ms = jnp.mean(x * x, axis=-1, keepdims=True)
    return x * lax.rsqrt(ms + RMS_EPS) * g


def _row_parts(tm):
    part = tm // 2 if tm >= 2 * MXU_DIM else tm
    return [slice(r0, r0 + part) for r0 in range(0, tm, part)]


def _tail_kernel(x_ref, ya_ref, o0_ref, l0_ref, o1_ref, l1_ref, o2_ref, l2_ref, bg_ref, p_ref,
                 wout_ref, wgate_ref, wproj_ref, plen_ref, fnorm_ref, y_ref,
                 o1_scr, l1_scr, o2_scr, l2_scr, yb_scr):
    tm = x_ref.shape[0]
    for src, dst in ((o1_ref, o1_scr), (o2_ref, o2_scr)):
        d = src.shape[0]
        for r in range(d):
            for h in range(DIL_HEADS):
                dst[h, pl.ds(r, tm // d, stride=d), :] = src[r, :, h * LANES:(h + 1) * LANES].astype(F32)
    for src, dst in ((l1_ref, l1_scr), (l2_ref, l2_scr)):
        d = src.shape[0]
        for r in range(d):
            dst[pl.ds(r, tm // d, stride=d), :] = src[r]
    for rows in _row_parts(tm):
        n = rows.stop - rows.start
        l0, l1, l2 = l0_ref[rows, :], l1_scr[rows, :], l2_scr[rows, :]
        mx = jnp.maximum(jnp.maximum(l0, l1), l2)
        e0, e1, e2 = jnp.exp(l0 - mx), jnp.exp(l1 - mx), jnp.exp(l2 - mx)
        inv = 1.0 / (e0 + e1 + e2)
        w0, w1, w2 = e0 * inv, e1 * inv, e2 * inv
        for h in range(DIL_HEADS):
            slab = slice(h * LANES, (h + 1) * LANES)
            wide = lambda w: jnp.broadcast_to(w[:, h * LSE_LANES:h * LSE_LANES + 1], (n, LANES))
            ob = wide(w0) * o0_ref[rows, slab] + wide(w1) * o1_scr[h, rows, :] + wide(w2) * o2_scr[h, rows, :]
            bg = bg_ref[rows, slab]
            yb_scr[rows, slab] = (ob * (bg / (1.0 + jnp.exp(-bg)))).astype(BF16)
        hid = (x_ref[rows, :]
               + jnp.dot(ya_ref[rows, :], wout_ref[:DIFF_WIDTH, :], preferred_element_type=F32)
               + jnp.dot(yb_scr[rows, :], wout_ref[DIFF_WIDTH:, :], preferred_element_type=F32))
        nrm = _rms(hid, plen_ref[...]).astype(BF16)
        gate = 1.0 / (1.0 + jnp.exp(-jnp.dot(nrm, wgate_ref[...], preferred_element_type=F32)))
        hid = hid + gate * jnp.dot(p_ref[rows, :].astype(BF16), wproj_ref[...], preferred_element_type=F32)
        y_ref[rows, :] = _rms(hid, fnorm_ref[...])


def _tail(x, ya, o0, l0, o1, l1, o2, l2, bg, p, wout, wgate, wproj, plen, fnorm):
    batch, seq, _ = x.shape
    tm = min(ROW_TILE, seq)
    row = lambda b, i: (b, i, 0)
    full = lambda a: pl.BlockSpec(a.shape, lambda b, i: (0,) * a.ndim)
    wide = pl.BlockSpec((None, tm, D_MODEL), row)
    half = pl.BlockSpec((None, tm, DIL_WIDTH), row)
    lse = pl.BlockSpec((None, tm, LANES), row)

    def res(a):
        d, width = a.shape[1], a.shape[3]
        return pl.BlockSpec((None, d, tm // d, width), lambda b, i: (b, 0, i, 0))

    return pl.pallas_call(
        _tail_kernel,
        grid=(batch, seq // tm),
        in_specs=[wide, half, half, lse, res(o1), res(l1), res(o2), res(l2), half,
                  pl.BlockSpec((None, tm, PLE_DIM), row),
                  full(wout), full(wgate), full(wproj), full(plen), full(fnorm)],
        out_specs=wide,
        out_shape=jax.ShapeDtypeStruct((batch, seq, D_MODEL), F32),
        scratch_shapes=[pltpu.VMEM((DIL_HEADS, tm, LANES), F32), pltpu.VMEM((tm, LANES), F32),
                        pltpu.VMEM((DIL_HEADS, tm, LANES), F32), pltpu.VMEM((tm, LANES), F32),
                        pltpu.VMEM((tm, DIL_WIDTH), BF16)],
        compiler_params=_cparams(("parallel", "parallel")),
        name="tail",
    )(x, ya, o0, l0, o1, l1, o2, l2, bg, p, wout, wgate, wproj, plen, fnorm)


def _layer(x, p, layer_idx, tabs, norm_mix, w_in, lam_params, subln, wout, wgate, wproj, plen, fnorm):
    batch, seq, _ = x.shape
    tabs_a, tabs_b = tabs
    lam_init = 0.8 - 0.6 * math.exp(-0.3 * layer_idx)

    qt, k, vt, ag = _proj_diff(x, norm_mix, w_in, tabs_a)
    ya = _diff_attention(lam_params, subln, qt, k, vt, ag, lam_init)

    outs, lses, bg = [], [], None
    for g, (_, dilation) in enumerate(DIL_PATTERNS):
        res = _proj_dil(x, norm_mix, w_in, tabs_b, g, dilation, with_gate=(g == 0))
        if g == 0:
            bg = res[3]
        sub = seq // dilation
        q, kk, v = (a.reshape(batch * seq, DIL_WIDTH) for a in res[:3])
        o, lse = _band_attention(q, kk, v, sub)
        outs.append(o.reshape(batch, dilation, sub, DIL_WIDTH))
        lses.append(lse.reshape(batch, dilation, sub, LANES))

    return _tail(x, ya, outs[0].reshape(batch, seq, DIL_WIDTH), lses[0].reshape(batch, seq, LANES),
                 outs[1], lses[1], outs[2], lses[2], bg, p, wout, wgate, wproj, plen, fnorm)


def kernel(x_prompt, x_sample, p_prompt, p_sample, norm_mix, w_in, lam_q1, lam_k1, lam_q2, lam_k2,
           subln, w_out, ple_norm, w_ple_gate, w_ple_proj, final_norm):
    depth = w_in.shape[0]
    assert depth == 1, "the fused tail applies the final norm, so it handles a single layer"
    i = 0
    seqs = (x_prompt.shape[1], x_sample.shape[1])
    tm = _proj_tile(min(seqs))
    assert all(_proj_tile(n) == tm for n in seqs)
    tabs = _rope_tables(tm, max(seqs) // tm)
    lam_params = jnp.stack([lam_q1[i], lam_k1[i], lam_q2[i], lam_k2[i]], axis=0)
    args = (tabs, norm_mix[i][None, :], w_in[i].astype(BF16), lam_params, subln[i][None, :],
            w_out[i].astype(BF16), w_ple_gate[i].astype(BF16), w_ple_proj[i].astype(BF16),
            ple_norm[i][None, :], final_norm[None, :])
    y_prompt = _layer(x_prompt, p_prompt[i], i, *args)
    y_sample = _layer(x_sample, p_sample[i], i, *args)
    return (y_prompt, y_sample)
```

```python
import functools
import math

import jax
import jax.numpy as jnp
import numpy as np
from jax import lax
from jax.experimental import pallas as pl
from jax.experimental.pallas import tpu as pltpu

F32 = jnp.float32
BF16 = jnp.bfloat16

D_MODEL = 2048
PLE_DIM = 256
ROPE_THETA = 10000.0
RMS_EPS = 1e-6
NEG = -1e30
LOG2E = math.log2(math.e)
LN2 = math.log(2.0)

DIFF_HEADS = 4
DIFF_QK_DIM = 64
DIFF_V_DIM = 128
DIFF_WIDTH = DIFF_HEADS * DIFF_V_DIM
DIL_PATTERNS = ((128, 1), (512, 4), (2048, 16))
DIL_HEADS = 4
DIL_HEAD_DIM = 128
DIL_WIDTH = DIL_HEADS * DIL_HEAD_DIM
DIL_RADIUS = 64
PERM_MIN_DILATION = 16
BAND_UNIT = 128
BAND_ROWS = 4096
BAND_MIN_STEPS = 4
ROW_TILE = 512
ATTN_BLOCK = 512
assert all(w // (2 * d) == DIL_RADIUS for w, d in DIL_PATTERNS)

LANES = 128
MXU_DIM = 256
LSE_LANES = LANES // DIL_HEADS
BF16_SUBLANES = 16
DIFF_Q_GROUP = 16
VT_ROWS = DIFF_V_DIM + BF16_SUBLANES
VMEM_LIMIT_BYTES = 56 * 1024 * 1024


def _cparams(semantics):
    return pltpu.CompilerParams(dimension_semantics=semantics, vmem_limit_bytes=VMEM_LIMIT_BYTES)


def _rope_tables(tm, n_blocks):
    def layout(dim, reps):
        inv = np.float32(ROPE_THETA) ** (-np.arange(0, dim, 2, dtype=np.float32) / np.float32(dim))
        freq = np.tile(inv.astype(np.float64), reps)[None, :]
        base = (np.arange(n_blocks, dtype=np.float64) * tm)[:, None] * freq
        row = np.arange(tm, dtype=np.float64)[:, None] * freq
        blk = [jnp.asarray(f(base)[:, None, :], F32) for f in (np.cos, np.sin)]
        return blk + [jnp.asarray(f(row), F32) for f in (np.cos, np.sin)]

    return layout(DIFF_QK_DIM, LANES // (DIFF_QK_DIM // 2)), layout(DIL_HEAD_DIM, LANES // (DIL_HEAD_DIM // 2))


def _angle_add(cb_ref, sb_ref, cr_ref, sr_ref):
    cb, sb, cr, sr = cb_ref[...], sb_ref[...], cr_ref[...], sr_ref[...]
    return cb * cr - sb * sr, sb * cr + cb * sr


def _rope64(x, cos, sin_up, sin_dn):
    return x * cos + pltpu.roll(x, 96, 1) * sin_up + pltpu.roll(x, 32, 1) * sin_dn


def _rope128(x, cos, sin):
    return x * cos + pltpu.roll(x, 64, 1) * sin


def _normalize(x_ref, g_ref):
    x = x_ref[...]
    ms = jnp.mean(x * x, axis=-1, keepdims=True)
    return (x * lax.rsqrt(ms + RMS_EPS) * g_ref[...]).astype(BF16)


def _proj_diff_kernel(x_ref, g_ref, w_ref, cb_ref, sb_ref, cr_ref, sr_ref, qt_ref, k_ref, vt_ref, gate_ref):
    z = jnp.dot(_normalize(x_ref, g_ref), w_ref[...], preferred_element_type=F32)
    cos, sin = _angle_add(cb_ref, sb_ref, cr_ref, sr_ref)
    half = lax.broadcasted_iota(jnp.int32, (1, LANES), 1) % DIFF_QK_DIM < DIFF_QK_DIM // 2
    sup = sin * jnp.where(half, -1.0, 0.0)
    sdn = sin * jnp.where(half, 0.0, 1.0)
    qscale = DIFF_QK_DIM**-0.5 * LOG2E
    for h in range(DIFF_HEADS):
        lo = h * LANES
        q = _rope64(z[:, lo:lo + LANES], cos, sup, sdn) * qscale
        qt_ref[lo:lo + LANES, :] = q.T.astype(BF16)
        k = _rope64(z[:, DIFF_WIDTH + lo:DIFF_WIDTH + lo + LANES], cos, sup, sdn)
        k_ref[:, lo:lo + LANES] = k.astype(BF16)
        v = z[:, 2 * DIFF_WIDTH + lo:2 * DIFF_WIDTH + lo + LANES]
        vlo = h * VT_ROWS
        vt_ref[vlo:vlo + LANES, :] = v.T.astype(BF16)
        vt_ref[vlo + LANES:vlo + VT_ROWS, :] = jnp.ones((VT_ROWS - LANES, v.shape[0]), BF16)
    gate_ref[...] = z[:, 3 * DIFF_WIDTH:]


def _proj_dil_kernel(x_ref, g_ref, *refs, dilation, with_gate, use_perm):
    n_w = 4 if with_gate else 3
    w_refs, tab_refs, refs = refs[:n_w], refs[n_w:n_w + 4], refs[n_w + 4:]
    perm_ref = None
    if use_perm:
        perm_ref, refs = refs[0], refs[1:]
    q_ref, k_ref, v_ref = refs[:3]
    u = _normalize(x_ref, g_ref)
    zq, zk, zv = (jnp.dot(u, w[...], preferred_element_type=F32) for w in w_refs[:3])
    cos, sin = _angle_add(*tab_refs)
    first = lax.broadcasted_iota(jnp.int32, (1, LANES), 1) < DIL_HEAD_DIM // 2
    sin = sin * jnp.where(first, -1.0, 1.0)
    qscale = DIL_HEAD_DIM**-0.5 * LOG2E
    tm = u.shape[0]
    rows = tm // dilation
    if dilation == 1:
        for h in range(DIL_HEADS):
            lanes = slice(h * LANES, (h + 1) * LANES)
            q_ref[0, :, lanes] = (_rope128(zq[:, lanes], cos, sin) * qscale).astype(BF16)
            k_ref[0, :, lanes] = _rope128(zk[:, lanes], cos, sin).astype(BF16)
        v_ref[0] = zv.astype(BF16)
    elif perm_ref is not None:
        zb = jnp.concatenate(
            [jnp.concatenate([(_rope128(zq[:, h * LANES:(h + 1) * LANES], cos, sin) * qscale).astype(BF16)
                              for h in range(DIL_HEADS)], axis=1),
             jnp.concatenate([_rope128(zk[:, h * LANES:(h + 1) * LANES], cos, sin).astype(BF16)
                              for h in range(DIL_HEADS)], axis=1),
             zv.astype(BF16)], axis=1)
        part = perm_ref.shape[0]
        sub = part // dilation
        for p0 in range(0, tm, part):
            zp = jnp.dot(perm_ref[...], zb[p0:p0 + part], preferred_element_type=F32).astype(BF16)
            for r in range(dilation):
                for j, out_ref in enumerate((q_ref, k_ref, v_ref)):
                    out_ref[r, p0 // dilation:p0 // dilation + sub, :] = (
                        zp[r * sub:(r + 1) * sub, j * DIL_WIDTH:(j + 1) * DIL_WIDTH])
    else:
        stage_ref = refs[-1]
        for h in range(DIL_HEADS):
            lanes = slice(h * LANES, (h + 1) * LANES)
            stage_ref[h] = _rope128(zq[:, lanes], cos, sin) * qscale
            stage_ref[DIL_HEADS + h] = _rope128(zk[:, lanes], cos, sin)
            stage_ref[2 * DIL_HEADS + h] = zv[:, lanes]
        for r in range(dilation):
            for j, out_ref in enumerate((q_ref, k_ref, v_ref)):
                for h in range(DIL_HEADS):
                    blk = stage_ref[j * DIL_HEADS + h, pl.ds(r, rows, stride=dilation), :]
                    out_ref[r, :, h * LANES:(h + 1) * LANES] = blk.astype(BF16)
    if with_gate:
        refs[3][...] = jnp.dot(u, w_refs[3][...], preferred_element_type=F32)


def _proj_tile(seq):
    return min(ROW_TILE, seq)


def _rope_specs(tm):
    blk = pl.BlockSpec((None, 1, LANES), lambda b, i: (i, 0, 0))
    row = pl.BlockSpec((tm, LANES), lambda b, i: (0, 0))
    return [blk, blk, row, row]


def _proj_diff(x, g, w, tabs):
    batch, seq, _ = x.shape
    tm = _proj_tile(seq)
    row = lambda b, i: (b, i, 0)
    col = lambda b, i: (b, 0, i)
    return pl.pallas_call(
        _proj_diff_kernel,
        grid=(batch, seq // tm),
        in_specs=[
            pl.BlockSpec((None, tm, D_MODEL), row),
            pl.BlockSpec((1, D_MODEL), lambda b, i: (0, 0)),
            pl.BlockSpec((D_MODEL, 4 * DIFF_WIDTH), lambda b, i: (0, 0)),
            *_rope_specs(tm),
        ],
        out_specs=[
            pl.BlockSpec((None, DIFF_WIDTH, tm), col),
            pl.BlockSpec((None, tm, DIFF_WIDTH), row),
            pl.BlockSpec((None, DIFF_HEADS * VT_ROWS, tm), col),
            pl.BlockSpec((None, tm, DIFF_WIDTH), row),
        ],
        out_shape=[
            jax.ShapeDtypeStruct((batch, DIFF_WIDTH, seq), BF16),
            jax.ShapeDtypeStruct((batch, seq, DIFF_WIDTH), BF16),
            jax.ShapeDtypeStruct((batch, DIFF_HEADS * VT_ROWS, seq), BF16),
            jax.ShapeDtypeStruct((batch, seq, DIFF_WIDTH), F32),
        ],
        compiler_params=_cparams(("parallel", "parallel")),
        name="proj_diff",
    )(x, g, w, *tabs)


def _proj_dil(x, g, w, tabs, group, dilation, with_gate):
    batch, seq, _ = x.shape
    tm = _proj_tile(seq)
    rows = tm // dilation
    sub = seq // dilation
    row = lambda b, i: (b, i, 0)
    res = lambda b, i: (b, 0, i, 0)
    res_spec = pl.BlockSpec((None, dilation, rows, DIL_WIDTH), res)
    res_shape = jax.ShapeDtypeStruct((batch, dilation, sub, DIL_WIDTH), BF16)
    out_specs = [res_spec, res_spec, res_spec]
    out_shape = [res_shape, res_shape, res_shape]
    if with_gate:
        out_specs.append(pl.BlockSpec((None, tm, DIL_WIDTH), row))
        out_shape.append(jax.ShapeDtypeStruct((batch, seq, DIL_WIDTH), F32))
    use_perm = dilation >= PERM_MIN_DILATION
    scratch = [] if dilation == 1 or use_perm else [pltpu.VMEM((3 * DIL_HEADS, tm, LANES), F32)]
    perm_args, perm_specs = [], []
    if use_perm:
        part = min(MXU_DIM, tm)
        sub = part // dilation
        src = (np.arange(part) % sub) * dilation + np.arange(part) // sub
        perm_args = [jnp.asarray(np.eye(part, dtype=np.float32)[src], BF16)]
        perm_specs = [pl.BlockSpec((part, part), lambda b, i: (0, 0))]
    groups = len(DIL_PATTERNS)
    col_blocks = [4 + j * groups + group for j in range(3)] + ([4 + 3 * groups] if with_gate else [])
    w_specs = [pl.BlockSpec((D_MODEL, DIL_WIDTH), functools.partial(lambda b, i, c: (0, c), c=c))
               for c in col_blocks]
    return pl.pallas_call(
        functools.partial(_proj_dil_kernel, dilation=dilation, with_gate=with_gate, use_perm=use_perm),
        grid=(batch, seq // tm),
        in_specs=[
            pl.BlockSpec((None, tm, D_MODEL), row),
            pl.BlockSpec((1, D_MODEL), lambda b, i: (0, 0)),
            *w_specs,
            *_rope_specs(tm),
            *perm_specs,
        ],
        out_specs=out_specs,
        out_shape=out_shape,
        scratch_shapes=scratch,
        compiler_params=_cparams(("parallel", "parallel")),
        name=f"proj_dil{dilation}",
    )(x, g, *([w] * len(w_specs)), *tabs, *perm_args)


def _diff_attn_kernel(lam_ref, subln_ref, qt_ref, k_ref, vt_ref, gate_ref, o_ref,
                      w_scr, wn_scr, s_scr, bm_scr, m_scr, acc_scr, fin_scr, *, tq, tk, unroll, lam_init):
    nk = k_ref.shape[0] // tk
    n_q = qt_ref.shape[1] // tq
    n_body = nk // unroll

    def load_w(dst, qi):
        qt = qt_ref[:, pl.ds(pl.multiple_of(qi * tq, tq), tq)]
        row = lax.broadcasted_iota(jnp.int32, qt.shape, 0)
        zero = jnp.zeros_like(qt)
        dst[0] = jnp.where(row < DIFF_QK_DIM, qt, zero)
        dst[1] = jnp.where(row >= DIFF_QK_DIM, qt, zero)

    def reset():
        m_scr[...] = jnp.full(m_scr.shape, NEG, F32)
        acc_scr[...] = jnp.zeros(acc_scr.shape, F32)

    def scores(w, j, slot):
        start = pl.multiple_of(j * tk, tk)
        kb = k_ref[pl.ds(start, tk), :]
        for c in range(2):
            s = jnp.dot(kb, w[c], preferred_element_type=F32)
            s_scr[slot, c] = s
            bm_scr[slot, c] = jnp.max(s, axis=0, keepdims=True)

    def accumulate(j, slot):
        start = pl.multiple_of(j * tk, tk)
        vb = vt_ref[:, pl.ds(start, tk)]
        for c in range(2):
            m_old = m_scr[c]
            m_new = jnp.maximum(m_old, bm_scr[slot, c])
            alpha = jnp.exp2(m_old - m_new)
            p = jnp.exp2(s_scr[slot, c] - m_new).astype(BF16)
            acc_scr[c] = alpha * acc_scr[c] + jnp.dot(vb, p, preferred_element_type=F32)
            m_scr[c] = m_new

    def steps(j, to_next_query_block):
        for u in range(unroll):
            if to_next_query_block and u == unroll - 1:
                scores(wn_scr, 0, 0)
            else:
                scores(w_scr, j + u + 1, (u + 1) % 2)
            accumulate(j + u, u % 2)

    def output(qi):
        rows = pl.ds(pl.multiple_of(qi * tq, tq), tq)
        lam_p = lam_ref[...]
        lam = (jnp.exp(jnp.sum(lam_p[0:1] * lam_p[1:2], axis=-1, keepdims=True))
               - jnp.exp(jnp.sum(lam_p[2:3] * lam_p[3:4], axis=-1, keepdims=True)) + lam_init)
        a0, a1 = fin_scr[0], fin_scr[1]
        o_t = (a0[:DIFF_V_DIM] * (1.0 / a0[DIFF_V_DIM:DIFF_V_DIM + 1])
               - lam * (a1[:DIFF_V_DIM] * (1.0 / a1[DIFF_V_DIM:DIFF_V_DIM + 1])))
        ms = jnp.mean(o_t * o_t, axis=0, keepdims=True)
        o = (o_t * lax.rsqrt(ms + RMS_EPS)).T
        g = gate_ref[rows, :]
        silu = g / (1.0 + jnp.exp(-g))
        o_ref[rows, :] = (o * (subln_ref[...] * (1.0 - lam_init)) * silu).astype(o_ref.dtype)

    load_w(w_scr, 0)
    reset()
    fin_scr[...] = jnp.ones(fin_scr.shape, F32)
    scores(w_scr, 0, 0)

    def query_block(qi, carry):
        output(jnp.maximum(qi - 1, 0))
        steps(0, False)
        if n_body > 2:
            def body(i, c):
                steps(unroll * i, False)
                return c
            lax.fori_loop(1, n_body - 1, body, 0)
        load_w(wn_scr, jnp.minimum(qi + 1, n_q - 1))
        steps(nk - unroll, True)
        fin_scr[...] = acc_scr[...]
        reset()
        w_scr[...] = wn_scr[...]
        return carry

    lax.fori_loop(0, n_q, query_block, 0)
    output(n_q - 1)


def _diff_attention(lam_params, subln, qt, k, vt, gate, lam_init):
    batch, seq, _ = k.shape
    tq = min(ATTN_BLOCK, seq)
    tk = min(ATTN_BLOCK, seq // 2)
    unroll = next(u for u in (16, 8, 4, 2) if seq // tk >= 2 * u)
    assert seq % (unroll * tk) == 0 and seq // (unroll * tk) >= 2
    tqg = min(DIFF_Q_GROUP * tq, seq)
    return pl.pallas_call(
        functools.partial(_diff_attn_kernel, tq=tq, tk=tk, unroll=unroll, lam_init=lam_init),
        grid=(batch, DIFF_HEADS, seq // tqg),
        in_specs=[
            pl.BlockSpec(lam_params.shape, lambda b, h, i: (0, 0)),
            pl.BlockSpec(subln.shape, lambda b, h, i: (0, 0)),
            pl.BlockSpec((None, LANES, tqg), lambda b, h, i: (b, h, i)),
            pl.BlockSpec((None, seq, LANES), lambda b, h, i: (b, 0, h)),
            pl.BlockSpec((None, VT_ROWS, seq), lambda b, h, i: (b, h, 0)),
            pl.BlockSpec((None, tqg, LANES), lambda b, h, i: (b, i, h)),
        ],
        out_specs=pl.BlockSpec((None, tqg, LANES), lambda b, h, i: (b, i, h)),
        out_shape=jax.ShapeDtypeStruct((batch, seq, DIFF_WIDTH), BF16),
        scratch_shapes=[
            pltpu.VMEM((2, LANES, tq), BF16),
            pltpu.VMEM((2, LANES, tq), BF16),
            pltpu.VMEM((2, 2, tk, tq), F32),
            pltpu.VMEM((2, 2, 1, tq), F32),
            pltpu.VMEM((2, 1, tq), F32),
            pltpu.VMEM((2, VT_ROWS, tq), F32),
            pltpu.VMEM((2, VT_ROWS, tq), F32),
        ],
        compiler_params=_cparams(("parallel", "parallel", "parallel")),
        name="diff_attn",
    )(lam_params, subln, qt, k, vt, gate)


def _band_window(prev_ref, cur_ref, next_ref, unit, n_units, lo):
    r, q = DIL_RADIUS, BAND_UNIT
    lanes = slice(lo, lo + LANES)
    pieces = []
    if unit == 0:
        pieces.append(prev_ref[:, lanes])
    first = max(unit * q - r, 0)
    last = min(unit * q + q + r, n_units * q)
    pieces.append(cur_ref[first:last, lanes])
    if unit == n_units - 1:
        pieces.append(next_ref[:, lanes])
    return pieces[0] if len(pieces) == 1 else jnp.concatenate(pieces, axis=0)


def _band_attn_kernel(q_ref, kp_ref, kc_ref, kn_ref, vp_ref, vc_ref, vn_ref, o_ref, lse_ref, *, sub_len):
    t_rows = q_ref.shape[0]
    n_units = t_rows // BAND_UNIT
    period = min(sub_len, t_rows)
    tw = BAND_UNIT + 2 * DIL_RADIUS
    i = pl.program_id(0)
    row = lax.broadcasted_iota(jnp.int32, (BAND_UNIT, tw), 0)
    col = lax.broadcasted_iota(jnp.int32, (BAND_UNIT, tw), 1)
    rel = col - row
    band = jnp.where(rel >= 0, 0.0, NEG)
    band = jnp.where(rel <= 2 * DIL_RADIUS, band, NEG).astype(F32)
    for u in range(n_units):
        bias = band
        offset = (u * BAND_UNIT) % period
        may_start, may_end = offset == 0, offset == period - BAND_UNIT
        if may_start or may_end:
            kpos = lax.rem(i * t_rows + u * BAND_UNIT, sub_len) - DIL_RADIUS + col
            if may_start:
                bias = jnp.where(kpos >= 0, bias, NEG)
            if may_end:
                bias = jnp.where(kpos < sub_len, bias, NEG)
        rows = slice(u * BAND_UNIT, (u + 1) * BAND_UNIT)
        for h in range(DIL_HEADS):
            lo = h * LANES
            kwin = _band_window(kp_ref, kc_ref, kn_ref, u, n_units, lo)
            vwin = _band_window(vp_ref, vc_ref, vn_ref, u, n_units, lo)
            s = lax.dot_general(q_ref[rows, lo:lo + LANES], kwin,
                                (((1,), (1,)), ((), ())), preferred_element_type=F32)
            s = s + bias
            m = jnp.max(s, axis=-1, keepdims=True)
            p = jnp.exp2(s - m)
            den = jnp.sum(p, axis=-1, keepdims=True)
            o = jnp.dot(p.astype(BF16), vwin, preferred_element_type=F32)
            o_ref[rows, lo:lo + LANES] = (o * (1.0 / den)).astype(o_ref.dtype)
            lse = (m + jnp.log2(den)) * LN2
            lse_ref[rows, h * LSE_LANES:(h + 1) * LSE_LANES] = jnp.broadcast_to(lse, (BAND_UNIT, LSE_LANES))


def _band_attention(q, k, v, sub_len):
    n_rows = q.shape[0]
    t_rows = min(BAND_ROWS, n_rows // BAND_MIN_STEPS)
    assert n_rows % t_rows == 0 and sub_len % BAND_UNIT == 0
    assert t_rows % sub_len == 0 or sub_len % t_rows == 0
    per = t_rows // DIL_RADIUS
    nhalo = n_rows // DIL_RADIUS
    cur = pl.BlockSpec((t_rows, DIL_WIDTH), lambda i: (i, 0))
    prev = pl.BlockSpec((DIL_RADIUS, DIL_WIDTH), lambda i: (jnp.maximum(i * per - 1, 0), 0))
    nxt = pl.BlockSpec((DIL_RADIUS, DIL_WIDTH), lambda i: (jnp.minimum((i + 1) * per, nhalo - 1), 0))
    return pl.pallas_call(
        functools.partial(_band_attn_kernel, sub_len=sub_len),
        grid=(n_rows // t_rows,),
        in_specs=[cur, prev, cur, nxt, prev, cur, nxt],
        out_specs=[cur, pl.BlockSpec((t_rows, LANES), lambda i: (i, 0))],
        out_shape=[jax.ShapeDtypeStruct((n_rows, DIL_WIDTH), BF16), jax.ShapeDtypeStruct((n_rows, LANES), F32)],
        compiler_params=_cparams(("parallel",)),
        name=f"band_attn{sub_len}",
    )(q, k, k, k, v, v, v)


def _rms(x, g):
    ms = jnp.mean(x * x, axis=-1, keepdims=True)
    return x * lax.rsqrt(ms + RMS_EPS) * g


def _row_parts(tm):
    part = tm // 2 if tm >= 2 * MXU_DIM else tm
    return [slice(r0, r0 + part) for r0 in range(0, tm, part)]


def _tail_kernel(x_ref, ya_ref, o0_ref, l0_ref, o1_ref, l1_ref, o2_ref, l2_ref, bg_ref, p_ref,
                 wout_ref, wgate_ref, wproj_ref, plen_ref, fnorm_ref, y_ref,
                 o1_scr, l1_scr, o2_scr, l2_scr, yb_scr):
    tm = x_ref.shape[0]
    for src, dst in ((o1_ref, o1_scr), (o2_ref, o2_scr)):
        d = src.shape[0]
        for r in range(d):
            for h in range(DIL_HEADS):
                dst[h, pl.ds(r, tm // d, stride=d), :] = src[r, :, h * LANES:(h + 1) * LANES].astype(F32)
    for src, dst in ((l1_ref, l1_scr), (l2_ref, l2_scr)):
        d = src.shape[0]
        for r in range(d):
            dst[pl.ds(r, tm // d, stride=d), :] = src[r]
    for rows in _row_parts(tm):
        n = rows.stop - rows.start
        l0, l1, l2 = l0_ref[rows, :], l1_scr[rows, :], l2_scr[rows, :]
        mx = jnp.maximum(jnp.maximum(l0, l1), l2)
        e0, e1, e2 = jnp.exp(l0 - mx), jnp.exp(l1 - mx), jnp.exp(l2 - mx)
        inv = 1.0 / (e0 + e1 + e2)
        w0, w1, w2 = e0 * inv, e1 * inv, e2 * inv
        for h in range(DIL_HEADS):
            slab = slice(h * LANES, (h + 1) * LANES)
            wide = lambda w: jnp.broadcast_to(w[:, h * LSE_LANES:h * LSE_LANES + 1], (n, LANES))
            ob = wide(w0) * o0_ref[rows, slab] + wide(w1) * o1_scr[h, rows, :] + wide(w2) * o2_scr[h, rows, :]
            bg = bg_ref[rows, slab]
            yb_scr[rows, slab] = (ob * (bg / (1.0 + jnp.exp(-bg)))).astype(BF16)
        hid = (x_ref[rows, :]
               + jnp.dot(ya_ref[rows, :], wout_ref[:DIFF_WIDTH, :], preferred_element_type=F32)
               + jnp.dot(yb_scr[rows, :], wout_ref[DIFF_WIDTH:, :], preferred_element_type=F32))
        nrm = _rms(hid, plen_ref[...]).astype(BF16)
        gate = 1.0 / (1.0 + jnp.exp(-jnp.dot(nrm, wgate_ref[...], preferred_element_type=F32)))
        hid = hid + gate * jnp.dot(p_ref[rows, :].astype(BF16), wproj_ref[...], preferred_element_type=F32)
        y_ref[rows, :] = _rms(hid, fnorm_ref[...])


def _tail(x, ya, o0, l0, o1, l1, o2, l2, bg, p, wout, wgate, wproj, plen, fnorm):
    batch, seq, _ = x.shape
    tm = min(ROW_TILE, seq)
    row = lambda b, i: (b, i, 0)
    full = lambda a: pl.BlockSpec(a.shape, lambda b, i: (0,) * a.ndim)
    wide = pl.BlockSpec((None, tm, D_MODEL), row)
    half = pl.BlockSpec((None, tm, DIL_WIDTH), row)
    lse = pl.BlockSpec((None, tm, LANES), row)

    def res(a):
        d, width = a.shape[1], a.shape[3]
        return pl.BlockSpec((None, d, tm // d, width), lambda b, i: (b, 0, i, 0))

    return pl.pallas_call(
        _tail_kernel,
        grid=(batch, seq // tm),
        in_specs=[wide, half, half, lse, res(o1), res(l1), res(o2), res(l2), half,
                  pl.BlockSpec((None, tm, PLE_DIM), row),
                  full(wout), full(wgate), full(wproj), full(plen), full(fnorm)],
        out_specs=wide,
        out_shape=jax.ShapeDtypeStruct((batch, seq, D_MODEL), F32),
        scratch_shapes=[pltpu.VMEM((DIL_HEADS, tm, LANES), F32), pltpu.VMEM((tm, LANES), F32),
                        pltpu.VMEM((DIL_HEADS, tm, LANES), F32), pltpu.VMEM((tm, LANES), F32),
                        pltpu.VMEM((tm, DIL_WIDTH), BF16)],
        compiler_params=_cparams(("parallel", "parallel")),
        name="tail",
    )(x, ya, o0, l0, o1, l1, o2, l2, bg, p, wout, wgate, wproj, plen, fnorm)


def _layer(x, p, layer_idx, tabs, norm_mix, w_in, lam_params, subln, wout, wgate, wproj, plen, fnorm):
    batch, seq, _ = x.shape
    tabs_a, tabs_b = tabs
    lam_init = 0.8 - 0.6 * math.exp(-0.3 * layer_idx)

    qt, k, vt, ag = _proj_diff(x, norm_mix, w_in, tabs_a)
    ya = _diff_attention(lam_params, subln, qt, k, vt, ag, lam_init)

    outs, lses, bg = [], [], None
    for g, (_, dilation) in enumerate(DIL_PATTERNS):
        res = _proj_dil(x, norm_mix, w_in, tabs_b, g, dilation, with_gate=(g == 0))
        if g == 0:
            bg = res[3]
        sub = seq // dilation
        q, kk, v = (a.reshape(batch * seq, DIL_WIDTH) for a in res[:3])
        o, lse = _band_attention(q, kk, v, sub)
        outs.append(o.reshape(batch, dilation, sub, DIL_WIDTH))
        lses.append(lse.reshape(batch, dilation, sub, LANES))

    return _tail(x, ya, outs[0].reshape(batch, seq, DIL_WIDTH), lses[0].reshape(batch, seq, LANES),
                 outs[1], lses[1], outs[2], lses[2], bg, p, wout, wgate, wproj, plen, fnorm)


def kernel(x_prompt, x_sample, p_prompt, p_sample, norm_mix, w_in, lam_q1, lam_k1, lam_q2, lam_k2,
           subln, w_out, ple_norm, w_ple_gate, w_ple_proj, final_norm):
    depth = w_in.shape[0]
    assert depth == 1, "the fused tail applies the final norm, so it handles a single layer"
    i = 0
    seqs = (x_prompt.shape[1], x_sample.shape[1])
    tm = _proj_tile(min(seqs))
    assert all(_proj_tile(n) == tm for n in seqs)
    tabs = _rope_tables(tm, max(seqs) // tm)
    lam_params = jnp.stack([lam_q1[i], lam_k1[i], lam_q2[i], lam_k2[i]], axis=0)
    args = (tabs, norm_mix[i][None, :], w_in[i].astype(BF16), lam_params, subln[i][None, :],
            w_out[i].astype(BF16), w_ple_gate[i].astype(BF16), w_ple_proj[i].astype(BF16),
            ple_norm[i][None, :], final_norm[None, :])
    y_prompt = _layer(x_prompt, p_prompt[i], i, *args)
    y_sample = _layer(x_sample, p_sample[i], i, *args)
    return (y_prompt, y_sample)
```

```python
import functools
import math

import jax
import jax.numpy as jnp
import numpy as np
from jax import lax
from jax.experimental import pallas as pl
from jax.experimental.pallas import tpu as pltpu

F32 = jnp.float32
BF16 = jnp.bfloat16

D_MODEL = 2048
PLE_DIM = 256
ROPE_THETA = 10000.0
RMS_EPS = 1e-6
NEG = -1e30
LOG2E = math.log2(math.e)
LN2 = math.log(2.0)

DIFF_HEADS = 4
DIFF_QK_DIM = 64
DIFF_V_DIM = 128
DIFF_WIDTH = DIFF_HEADS * DIFF_V_DIM
DIL_PATTERNS = ((128, 1), (512, 4), (2048, 16))
DIL_HEADS = 4
DIL_HEAD_DIM = 128
DIL_WIDTH = DIL_HEADS * DIL_HEAD_DIM
DIL_RADIUS = 64
PERM_MIN_DILATION = 16
BAND_UNIT = 128
BAND_ROWS = 4096
BAND_MIN_STEPS = 4
ROW_TILE = 512
ATTN_BLOCK = 512
assert all(w // (2 * d) == DIL_RADIUS for w, d in DIL_PATTERNS)

LANES = 128
MXU_DIM = 256
LSE_LANES = LANES // DIL_HEADS
BF16_SUBLANES = 16
DIFF_Q_GROUP = 16
VT_ROWS = DIFF_V_DIM + BF16_SUBLANES
VMEM_LIMIT_BYTES = 56 * 1024 * 1024


def _cparams(semantics):
    return pltpu.CompilerParams(dimension_semantics=semantics, vmem_limit_bytes=VMEM_LIMIT_BYTES)


def _rope_tables(tm, n_blocks):
    def layout(dim, reps):
        inv = np.float32(ROPE_THETA) ** (-np.arange(0, dim, 2, dtype=np.float32) / np.float32(dim))
        freq = np.tile(inv.astype(np.float64), reps)[None, :]
        base = (np.arange(n_blocks, dtype=np.float64) * tm)[:, None] * freq
        row = np.arange(tm, dtype=np.float64)[:, None] * freq
        blk = [jnp.asarray(f(base)[:, None, :], F32) for f in (np.cos, np.sin)]
        return blk + [jnp.asarray(f(row), F32) for f in (np.cos, np.sin)]

    return layout(DIFF_QK_DIM, LANES // (DIFF_QK_DIM // 2)), layout(DIL_HEAD_DIM, LANES // (DIL_HEAD_DIM // 2))


def _angle_add(cb_ref, sb_ref, cr_ref, sr_ref):
    cb, sb, cr, sr = cb_ref[...], sb_ref[...], cr_ref[...], sr_ref[...]
    return cb * cr - sb * sr, sb * cr + cb * sr


def _rope64(x, cos, sin_up, sin_dn):
    return x * cos + pltpu.roll(x, 96, 1) * sin_up + pltpu.roll(x, 32, 1) * sin_dn


def _rope128(x, cos, sin):
    return x * cos + pltpu.roll(x, 64, 1) * sin


def _normalize(x_ref, g_ref):
    x = x_ref[...]
    ms = jnp.mean(x * x, axis=-1, keepdims=True)
    return (x * lax.rsqrt(ms + RMS_EPS) * g_ref[...]).astype(BF16)


def _proj_diff_kernel(x_ref, g_ref, w_ref, cb_ref, sb_ref, cr_ref, sr_ref, qt_ref, k_ref, vt_ref, gate_ref):
    z = jnp.dot(_normalize(x_ref, g_ref), w_ref[...], preferred_element_type=F32)
    cos, sin = _angle_add(cb_ref, sb_ref, cr_ref, sr_ref)
    half = lax.broadcasted_iota(jnp.int32, (1, LANES), 1) % DIFF_QK_DIM < DIFF_QK_DIM // 2
    sup = sin * jnp.where(half, -1.0, 0.0)
    sdn = sin * jnp.where(half, 0.0, 1.0)
    qscale = DIFF_QK_DIM**-0.5 * LOG2E
    for h in range(DIFF_HEADS):
        lo = h * LANES
        q = _rope64(z[:, lo:lo + LANES], cos, sup, sdn) * qscale
        qt_ref[lo:lo + LANES, :] = q.T.astype(BF16)
        k = _rope64(z[:, DIFF_WIDTH + lo:DIFF_WIDTH + lo + LANES], cos, sup, sdn)
        k_ref[:, lo:lo + LANES] = k.astype(BF16)
        v = z[:, 2 * DIFF_WIDTH + lo:2 * DIFF_WIDTH + lo + LANES]
        vlo = h * VT_ROWS
        vt_ref[vlo:vlo + LANES, :] = v.T.astype(BF16)
        vt_ref[vlo + LANES:vlo + VT_ROWS, :] = jnp.ones((VT_ROWS - LANES, v.shape[0]), BF16)
    gate_ref[...] = z[:, 3 * DIFF_WIDTH:]


def _proj_dil_kernel(x_ref, g_ref, *refs, dilation, with_gate, use_perm):
    n_w = 4 if with_gate else 3
    w_refs, tab_refs, refs = refs[:n_w], refs[n_w:n_w + 4], refs[n_w + 4:]
    perm_ref = None
    if use_perm:
        perm_ref, refs = refs[0], refs[1:]
    q_ref, k_ref, v_ref = refs[:3]
    u = _normalize(x_ref, g_ref)
    zq, zk, zv = (jnp.dot(u, w[...], preferred_element_type=F32) for w in w_refs[:3])
    cos, sin = _angle_add(*tab_refs)
    first = lax.broadcasted_iota(jnp.int32, (1, LANES), 1) < DIL_HEAD_DIM // 2
    sin = sin * jnp.where(first, -1.0, 1.0)
    qscale = DIL_HEAD_DIM**-0.5 * LOG2E
    tm = u.shape[0]
    rows = tm // dilation
    if dilation == 1:
        for h in range(DIL_HEADS):
            lanes = slice(h * LANES, (h + 1) * LANES)
            q_ref[0, :, lanes] = (_rope128(zq[:, lanes], cos, sin) * qscale).astype(BF16)
            k_ref[0, :, lanes] = _rope128(zk[:, lanes], cos, sin).astype(BF16)
        v_ref[0] = zv.astype(BF16)
    elif perm_ref is not None:
        zb = jnp.concatenate(
            [jnp.concatenate([(_rope128(zq[:, h * LANES:(h + 1) * LANES], cos, sin) * qscale).astype(BF16)
                              for h in range(DIL_HEADS)], axis=1),
             jnp.concatenate([_rope128(zk[:, h * LANES:(h + 1) * LANES], cos, sin).astype(BF16)
                              for h in range(DIL_HEADS)], axis=1),
             zv.astype(BF16)], axis=1)
        part = perm_ref.shape[0]
        sub = part // dilation
        for p0 in range(0, tm, part):
            zp = jnp.dot(perm_ref[...], zb[p0:p0 + part], preferred_element_type=F32).astype(BF16)
            for r in range(dilation):
                for j, out_ref in enumerate((q_ref, k_ref, v_ref)):
                    out_ref[r, p0 // dilation:p0 // dilation + sub, :] = (
                        zp[r * sub:(r + 1) * sub, j * DIL_WIDTH:(j + 1) * DIL_WIDTH])
    else:
        stage_ref = refs[-1]
        for h in range(DIL_HEADS):
            lanes = slice(h * LANES, (h + 1) * LANES)
            stage_ref[h] = _rope128(zq[:, lanes], cos, sin) * qscale
            stage_ref[DIL_HEADS + h] = _rope128(zk[:, lanes], cos, sin)
            stage_ref[2 * DIL_HEADS + h] = zv[:, lanes]
        for r in range(dilation):
            for j, out_ref in enumerate((q_ref, k_ref, v_ref)):
                for h in range(DIL_HEADS):
                    blk = stage_ref[j * DIL_HEADS + h, pl.ds(r, rows, stride=dilation), :]
                    out_ref[r, :, h * LANES:(h + 1) * LANES] = blk.astype(BF16)
    if with_gate:
        refs[3][...] = jnp.dot(u, w_refs[3][...], preferred_element_type=F32)


def _proj_tile(seq):
    return min(ROW_TILE, seq)


def _rope_specs(tm):
    blk = pl.BlockSpec((None, 1, LANES), lambda b, i: (i, 0, 0))
    row = pl.BlockSpec((tm, LANES), lambda b, i: (0, 0))
    return [blk, blk, row, row]


def _proj_diff(x, g, w, tabs):
    batch, seq, _ = x.shape
    tm = _proj_tile(seq)
    row = lambda b, i: (b, i, 0)
    col = lambda b, i: (b, 0, i)
    return pl.pallas_call(
        _proj_diff_kernel,
        grid=(batch, seq // tm),
        in_specs=[
            pl.BlockSpec((None, tm, D_MODEL), row),
            pl.BlockSpec((1, D_MODEL), lambda b, i: (0, 0)),
            pl.BlockSpec((D_MODEL, 4 * DIFF_WIDTH), lambda b, i: (0, 0)),
            *_rope_specs(tm),
        ],
        out_specs=[
            pl.BlockSpec((None, DIFF_WIDTH, tm), col),
            pl.BlockSpec((None, tm, DIFF_WIDTH), row),
            pl.BlockSpec((None, DIFF_HEADS * VT_ROWS, tm), col),
            pl.BlockSpec((None, tm, DIFF_WIDTH), row),
        ],
        out_shape=[
            jax.ShapeDtypeStruct((batch, DIFF_WIDTH, seq), BF16),
            jax.ShapeDtypeStruct((batch, seq, DIFF_WIDTH), BF16),
            jax.ShapeDtypeStruct((batch, DIFF_HEADS * VT_ROWS, seq), BF16),
            jax.ShapeDtypeStruct((batch, seq, DIFF_WIDTH), F32),
        ],
        compiler_params=_cparams(("parallel", "parallel")),
        name="proj_diff",
    )(x, g, w, *tabs)


def _proj_dil(x, g, w, tabs, group, dilation, with_gate):
    batch, seq, _ = x.shape
    tm = _proj_tile(seq)
    rows = tm // dilation
    sub = seq // dilation
    row = lambda b, i: (b, i, 0)
    res = lambda b, i: (b, 0, i, 0)
    res_spec = pl.BlockSpec((None, dilation, rows, DIL_WIDTH), res)
    res_shape = jax.ShapeDtypeStruct((batch, dilation, sub, DIL_WIDTH), BF16)
    out_specs = [res_spec, res_spec, res_spec]
    out_shape = [res_shape, res_shape, res_shape]
    if with_gate:
        out_specs.append(pl.BlockSpec((None, tm, DIL_WIDTH), row))
        out_shape.append(jax.ShapeDtypeStruct((batch, seq, DIL_WIDTH), F32))
    use_perm = dilation >= PERM_MIN_DILATION
    scratch = [] if dilation == 1 or use_perm else [pltpu.VMEM((3 * DIL_HEADS, tm, LANES), F32)]
    perm_args, perm_specs = [], []
    if use_perm:
        part = min(MXU_DIM, tm)
        sub = part // dilation
        src = (np.arange(part) % sub) * dilation + np.arange(part) // sub
        perm_args = [jnp.asarray(np.eye(part, dtype=np.float32)[src], BF16)]
        perm_specs = [pl.BlockSpec((part, part), lambda b, i: (0, 0))]
    groups = len(DIL_PATTERNS)
    col_blocks = [4 + j * groups + group for j in range(3)] + ([4 + 3 * groups] if with_gate else [])
    w_specs = [pl.BlockSpec((D_MODEL, DIL_WIDTH), functools.partial(lambda b, i, c: (0, c), c=c))
               for c in col_blocks]
    return pl.pallas_call(
        functools.partial(_proj_dil_kernel, dilation=dilation, with_gate=with_gate, use_perm=use_perm),
        grid=(batch, seq // tm),
        in_specs=[
            pl.BlockSpec((None, tm, D_MODEL), row),
            pl.BlockSpec((1, D_MODEL), lambda b, i: (0, 0)),
            *w_specs,
            *_rope_specs(tm),
            *perm_specs,
        ],
        out_specs=out_specs,
        out_shape=out_shape,
        scratch_shapes=scratch,
        compiler_params=_cparams(("parallel", "parallel")),
        name=f"proj_dil{dilation}",
    )(x, g, *([w] * len(w_specs)), *tabs, *perm_args)


def _diff_attn_kernel(lam_ref, subln_ref, qt_ref, k_ref, vt_ref, gate_ref, o_ref,
                      w_scr, wn_scr, s_scr, bm_scr, m_scr, acc_scr, fin_scr, *, tq, tk, unroll, lam_init):
    nk = k_ref.shape[0] // tk
    n_q = qt_ref.shape[1] // tq
    n_body = nk // unroll

    def load_w(dst, qi):
        qt = qt_ref[:, pl.ds(pl.multiple_of(qi * tq, tq), tq)]
        row = lax.broadcasted_iota(jnp.int32, qt.shape, 0)
        zero = jnp.zeros_like(qt)
        dst[0] = jnp.where(row < DIFF_QK_DIM, qt, zero)
        dst[1] = jnp.where(row >= DIFF_QK_DIM, qt, zero)

    def reset():
        m_scr[...] = jnp.full(m_scr.shape, NEG, F32)
        acc_scr[...] = jnp.zeros(acc_scr.shape, F32)

    def scores(w, j, slot):
        start = pl.multiple_of(j * tk, tk)
        kb = k_ref[pl.ds(start, tk), :]
        for c in range(2):
            s = jnp.dot(kb, w[c], preferred_element_type=F32)
            s_scr[slot, c] = s
            bm_scr[slot, c] = jnp.max(s, axis=0, keepdims=True)

    def accumulate(j, slot):
        start = pl.multiple_of(j * tk, tk)
        vb = vt_ref[:, pl.ds(start, tk)]
        for c in range(2):
            m_old = m_scr[c]
            m_new = jnp.maximum(m_old, bm_scr[slot, c])
            alpha = jnp.exp2(m_old - m_new)
            p = jnp.exp2(s_scr[slot, c] - m_new).astype(BF16)
            acc_scr[c] = alpha * acc_scr[c] + jnp.dot(vb, p, preferred_element_type=F32)
            m_scr[c] = m_new

    def steps(j, to_next_query_block):
        for u in range(unroll):
            if to_next_query_block and u == unroll - 1:
                scores(wn_scr, 0, 0)
            else:
                scores(w_scr, j + u + 1, (u + 1) % 2)
            accumulate(j + u, u % 2)

    def output(qi):
        rows = pl.ds(pl.multiple_of(qi * tq, tq), tq)
        lam_p = lam_ref[...]
        lam = (jnp.exp(jnp.sum(lam_p[0:1] * lam_p[1:2], axis=-1, keepdims=True))
               - jnp.exp(jnp.sum(lam_p[2:3] * lam_p[3:4], axis=-1, keepdims=True)) + lam_init)
        a0, a1 = fin_scr[0], fin_scr[1]
        o_t = (a0[:DIFF_V_DIM] * (1.0 / a0[DIFF_V_DIM:DIFF_V_DIM + 1])
               - lam * (a1[:DIFF_V_DIM] * (1.0 / a1[DIFF_V_DIM:DIFF_V_DIM + 1])))
        ms = jnp.mean(o_t * o_t, axis=0, keepdims=True)
        o = (o_t * lax.rsqrt(ms + RMS_EPS)).T
        g = gate_ref[rows, :]
        silu = g / (1.0 + jnp.exp(-g))
        o_ref[rows, :] = (o * (subln_ref[...] * (1.0 - lam_init)) * silu).astype(o_ref.dtype)

    load_w(w_scr, 0)
    reset()
    fin_scr[...] = jnp.ones(fin_scr.shape, F32)
    scores(w_scr, 0, 0)

    def query_block(qi, carry):
        output(jnp.maximum(qi - 1, 0))
        steps(0, False)
        if n_body > 2:
            def body(i, c):
                steps(unroll * i, False)
                return c
            lax.fori_loop(1, n_body - 1, body, 0)
        load_w(wn_scr, jnp.minimum(qi + 1, n_q - 1))
        steps(nk - unroll, True)
        fin_scr[...] = acc_scr[...]
        reset()
        w_scr[...] = wn_scr[...]
        return carry

    lax.fori_loop(0, n_q, query_block, 0)
    output(n_q - 1)


def _diff_attention(lam_params, subln, qt, k, vt, gate, lam_init):
    batch, seq, _ = k.shape
    tq = min(ATTN_BLOCK, seq)
    tk = min(ATTN_BLOCK, seq // 2)
    unroll = next(u for u in (16, 8, 4, 2) if seq // tk >= 2 * u)
    assert seq % (unroll * tk) == 0 and seq // (unroll * tk) >= 2
    tqg = min(DIFF_Q_GROUP * tq, seq)
    return pl.pallas_call(
        functools.partial(_diff_attn_kernel, tq=tq, tk=tk, unroll=unroll, lam_init=lam_init),
        grid=(batch, DIFF_HEADS, seq // tqg),
        in_specs=[
            pl.BlockSpec(lam_params.shape, lambda b, h, i: (0, 0)),
            pl.BlockSpec(subln.shape, lambda b, h, i: (0, 0)),
            pl.BlockSpec((None, LANES, tqg), lambda b, h, i: (b, h, i)),
            pl.BlockSpec((None, seq, LANES), lambda b, h, i: (b, 0, h)),
            pl.BlockSpec((None, VT_ROWS, seq), lambda b, h, i: (b, h, 0)),
            pl.BlockSpec((None, tqg, LANES), lambda b, h, i: (b, i, h)),
        ],
        out_specs=pl.BlockSpec((None, tqg, LANES), lambda b, h, i: (b, i, h)),
        out_shape=jax.ShapeDtypeStruct((batch, seq, DIFF_WIDTH), BF16),
        scratch_shapes=[
            pltpu.VMEM((2, LANES, tq), BF16),
            pltpu.VMEM((2, LANES, tq), BF16),
            pltpu.VMEM((2, 2, tk, tq), F32),
            pltpu.VMEM((2, 2, 1, tq), F32),
            pltpu.VMEM((2, 1, tq), F32),
            pltpu.VMEM((2, VT_ROWS, tq), F32),
            pltpu.VMEM((2, VT_ROWS, tq), F32),
        ],
        compiler_params=_cparams(("parallel", "parallel", "parallel")),
        name="diff_attn",
    )(lam_params, subln, qt, k, vt, gate)


def _band_window(prev_ref, cur_ref, next_ref, unit, n_units, lo):
    r, q = DIL_RADIUS, BAND_UNIT
    lanes = slice(lo, lo + LANES)
    pieces = []
    if unit == 0:
        pieces.append(prev_ref[:, lanes])
    first = max(unit * q - r, 0)
    last = min(unit * q + q + r, n_units * q)
    pieces.append(cur_ref[first:last, lanes])
    if unit == n_units - 1:
        pieces.append(next_ref[:, lanes])
    return pieces[0] if len(pieces) == 1 else jnp.concatenate(pieces, axis=0)


def _band_attn_kernel(q_ref, kp_ref, kc_ref, kn_ref, vp_ref, vc_ref, vn_ref, o_ref, lse_ref, *, sub_len):
    t_rows = q_ref.shape[0]
    n_units = t_rows // BAND_UNIT
    period = min(sub_len, t_rows)
    tw = BAND_UNIT + 2 * DIL_RADIUS
    i = pl.program_id(0)
    row = lax.broadcasted_iota(jnp.int32, (BAND_UNIT, tw), 0)
    col = lax.broadcasted_iota(jnp.int32, (BAND_UNIT, tw), 1)
    rel = col - row
    band = jnp.where(rel >= 0, 0.0, NEG)
    band = jnp.where(rel <= 2 * DIL_RADIUS, band, NEG).astype(F32)
    for u in range(n_units):
        bias = band
        offset = (u * BAND_UNIT) % period
        may_start, may_end = offset == 0, offset == period - BAND_UNIT
        if may_start or may_end:
            kpos = lax.rem(i * t_rows + u * BAND_UNIT, sub_len) - DIL_RADIUS + col
            if may_start:
                bias = jnp.where(kpos >= 0, bias, NEG)
            if may_end:
                bias = jnp.where(kpos < sub_len, bias, NEG)
        rows = slice(u * BAND_UNIT, (u + 1) * BAND_UNIT)
        for h in range(DIL_HEADS):
            lo = h * LANES
            kwin = _band_window(kp_ref, kc_ref, kn_ref, u, n_units, lo)
            vwin = _band_window(vp_ref, vc_ref, vn_ref, u, n_units, lo)
            s = lax.dot_general(q_ref[rows, lo:lo + LANES], kwin,
                                (((1,), (1,)), ((), ())), preferred_element_type=F32)
            s = s + bias
            m = jnp.max(s, axis=-1, keepdims=True)
            p = jnp.exp2(s - m)
            den = jnp.sum(p, axis=-1, keepdims=True)
            o = jnp.dot(p.astype(BF16), vwin, preferred_element_type=F32)
            o_ref[rows, lo:lo + LANES] = (o * (1.0 / den)).astype(o_ref.dtype)
            lse = (m + jnp.log2(den)) * LN2
            lse_ref[rows, h * LSE_LANES:(h + 1) * LSE_LANES] = jnp.broadcast_to(lse, (BAND_UNIT, LSE_LANES))


def _band_attention(q, k, v, sub_len):
    n_rows = q.shape[0]
    t_rows = min(BAND_ROWS, n_rows // BAND_MIN_STEPS)
    assert n_rows % t_rows == 0 and sub_len % BAND_UNIT == 0
    assert t_rows % sub_len == 0 or sub_len % t_rows == 0
    per = t_rows // DIL_RADIUS
    nhalo = n_rows // DIL_RADIUS
    cur = pl.BlockSpec((t_rows, DIL_WIDTH), lambda i: (i, 0))
    prev = pl.BlockSpec((DIL_RADIUS, DIL_WIDTH), lambda i: (jnp.maximum(i * per - 1, 0), 0))
    nxt = pl.BlockSpec((DIL_RADIUS, DIL_WIDTH), lambda i: (jnp.minimum((i + 1) * per, nhalo - 1), 0))
    return pl.pallas_call(
        functools.partial(_band_attn_kernel, sub_len=sub_len),
        grid=(n_rows // t_rows,),
        in_specs=[cur, prev, cur, nxt, prev, cur, nxt],
        out_specs=[cur, pl.BlockSpec((t_rows, LANES), lambda i: (i, 0))],
        out_shape=[jax.ShapeDtypeStruct((n_rows, DIL_WIDTH), BF16), jax.ShapeDtypeStruct((n_rows, LANES), F32)],
        compiler_params=_cparams(("parallel",)),
        name=f"band_attn{sub_len}",
    )(q, k, k, k, v, v, v)


def _rms(x, g):
    ms = jnp.mean(x * x, axis=-1, keepdims=True)
    return x * lax.rsqrt(ms + RMS_EPS) * g


def _row_parts(tm):
    part = tm // 2 if tm >= 2 * MXU_DIM else tm
    return [slice(r0, r0 + part) for r0 in range(0, tm, part)]


def _tail_kernel(x_ref, ya_ref, o0_ref, l0_ref, o1_ref, l1_ref, o2_ref, l2_ref, bg_ref, p_ref,
                 wout_ref, wgate_ref, wproj_ref, plen_ref, fnorm_ref, y_ref,
                 o1_scr, l1_scr, o2_scr, l2_scr, yb_scr):
    tm = x_ref.shape[0]
    for src, dst in ((o1_ref, o1_scr), (o2_ref, o2_scr)):
        d = src.shape[0]
        for r in range(d):
            for h in range(DIL_HEADS):
                dst[h, pl.ds(r, tm // d, stride=d), :] = src[r, :, h * LANES:(h + 1) * LANES].astype(F32)
    for src, dst in ((l1_ref, l1_scr), (l2_ref, l2_scr)):
        d = src.shape[0]
        for r in range(d):
            dst[pl.ds(r, tm // d, stride=d), :] = src[r]
    parts = _row_parts(tm)
    for rows in parts:
        n = rows.stop - rows.start
        l0, l1, l2 = l0_ref[rows, :], l1_scr[rows, :], l2_scr[rows, :]
        mx = jnp.maximum(jnp.maximum(l0, l1), l2)
        e0, e1, e2 = jnp.exp(l0 - mx), jnp.exp(l1 - mx), jnp.exp(l2 - mx)
        inv = 1.0 / (e0 + e1 + e2)
        w0, w1, w2 = e0 * inv, e1 * inv, e2 * inv
        for h in range(DIL_HEADS):
            slab = slice(h * LANES, (h + 1) * LANES)
            wide = lambda w: jnp.broadcast_to(w[:, h * LSE_LANES:h * LSE_LANES + 1], (n, LANES))
            ob = wide(w0) * o0_ref[rows, slab] + wide(w1) * o1_scr[h, rows, :] + wide(w2) * o2_scr[h, rows, :]
            bg = bg_ref[rows, slab]
            yb_scr[rows, slab] = (ob * (bg / (1.0 + jnp.exp(-bg)))).astype(BF16)
    hids = [x_ref[rows, :]
            + jnp.dot(ya_ref[rows, :], wout_ref[:DIFF_WIDTH, :], preferred_element_type=F32)
            + jnp.dot(yb_scr[rows, :], wout_ref[DIFF_WIDTH:, :], preferred_element_type=F32) for rows in parts]
    gates = [1.0 / (1.0 + jnp.exp(-jnp.dot(_rms(hid, plen_ref[...]).astype(BF16), wgate_ref[...],
                                           preferred_element_type=F32))) for hid in hids]
    for rows, hid, gate in zip(parts, hids, gates):
        hid = hid + gate * jnp.dot(p_ref[rows, :].astype(BF16), wproj_ref[...], preferred_element_type=F32)
        y_ref[rows, :] = _rms(hid, fnorm_ref[...])


def _tail(x, ya, o0, l0, o1, l1, o2, l2, bg, p, wout, wgate, wproj, plen, fnorm):
    batch, seq, _ = x.shape
    tm = min(ROW_TILE, seq)
    row = lambda b, i: (b, i, 0)
    full = lambda a: pl.BlockSpec(a.shape, lambda b, i: (0,) * a.ndim)
    wide = pl.BlockSpec((None, tm, D_MODEL), row)
    half = pl.BlockSpec((None, tm, DIL_WIDTH), row)
    lse = pl.BlockSpec((None, tm, LANES), row)

    def res(a):
        d, width = a.shape[1], a.shape[3]
        return pl.BlockSpec((None, d, tm // d, width), lambda b, i: (b, 0, i, 0))

    return pl.pallas_call(
        _tail_kernel,
        grid=(batch, seq // tm),
        in_specs=[wide, half, half, lse, res(o1), res(l1), res(o2), res(l2), half,
                  pl.BlockSpec((None, tm, PLE_DIM), row),
                  full(wout), full(wgate), full(wproj), full(plen), full(fnorm)],
        out_specs=wide,
        out_shape=jax.ShapeDtypeStruct((batch, seq, D_MODEL), F32),
        scratch_shapes=[pltpu.VMEM((DIL_HEADS, tm, LANES), F32), pltpu.VMEM((tm, LANES), F32),
                        pltpu.VMEM((DIL_HEADS, tm, LANES), F32), pltpu.VMEM((tm, LANES), F32),
                        pltpu.VMEM((tm, DIL_WIDTH), BF16)],
        compiler_params=_cparams(("parallel", "parallel")),
        name="tail",
    )(x, ya, o0, l0, o1, l1, o2, l2, bg, p, wout, wgate, wproj, plen, fnorm)


def _layer(x, p, layer_idx, tabs, norm_mix, w_in, lam_params, subln, wout, wgate, wproj, plen, fnorm):
    batch, seq, _ = x.shape
    tabs_a, tabs_b = tabs
    lam_init = 0.8 - 0.6 * math.exp(-0.3 * layer_idx)

    qt, k, vt, ag = _proj_diff(x, norm_mix, w_in, tabs_a)
    ya = _diff_attention(lam_params, subln, qt, k, vt, ag, lam_init)

    outs, lses, bg = [], [], None
    for g, (_, dilation) in enumerate(DIL_PATTERNS):
        res = _proj_dil(x, norm_mix, w_in, tabs_b, g, dilation, with_gate=(g == 0))
        if g == 0:
            bg = res[3]
        sub = seq // dilation
        q, kk, v = (a.reshape(batch * seq, DIL_WIDTH) for a in res[:3])
        o, lse = _band_attention(q, kk, v, sub)
        outs.append(o.reshape(batch, dilation, sub, DIL_WIDTH))
        lses.append(lse.reshape(batch, dilation, sub, LANES))

    return _tail(x, ya, outs[0].reshape(batch, seq, DIL_WIDTH), lses[0].reshape(batch, seq, LANES),
                 outs[1], lses[1], outs[2], lses[2], bg, p, wout, wgate, wproj, plen, fnorm)


def kernel(x_prompt, x_sample, p_prompt, p_sample, norm_mix, w_in, lam_q1, lam_k1, lam_q2, lam_k2,
           subln, w_out, ple_norm, w_ple_gate, w_ple_proj, final_norm):
    depth = w_in.shape[0]
    assert depth == 1, "the fused tail applies the final norm, so it handles a single layer"
    i = 0
    seqs = (x_prompt.shape[1], x_sample.shape[1])
    tm = _proj_tile(min(seqs))
    assert all(_proj_tile(n) == tm for n in seqs)
    tabs = _rope_tables(tm, max(seqs) // tm)
    lam_params = jnp.stack([lam_q1[i], lam_k1[i], lam_q2[i], lam_k2[i]], axis=0)
    args = (tabs, norm_mix[i][None, :], w_in[i].astype(BF16), lam_params, subln[i][None, :],
            w_out[i].astype(BF16), w_ple_gate[i].astype(BF16), w_ple_proj[i].astype(BF16),
            ple_norm[i][None, :], final_norm[None, :])
    y_prompt = _layer(x_prompt, p_prompt[i], i, *args)
    y_sample = _layer(x_sample, p_sample[i], i, *args)
    return (y_prompt, y_sample)
```
